```python
import jax, jax.numpy as jnp
from jax import lax
import numpy as np

D_MODEL = 2048
BATCH = 2
SEQ = 4096
DEPTH = 1

HEAD_DIM = 128
ATTN_Q_HEADS = 8
ATTN_KV_HEADS = 2
ATTN_GROUP = ATTN_Q_HEADS // ATTN_KV_HEADS
WINDOW = 128
ATTN_BLOCK = 128
ROPE_THETA = 10000.0
RET_HEADS = 8
RET_QK_DIM = 128
RET_V_DIM = 256
RET_CHUNK = 128
PEER_HEADS = 8
PEER_N_KEYS = 128
PEER_N_EXPERTS = PEER_N_KEYS * PEER_N_KEYS
PEER_TOPK = 16
PEER_KEY_DIM = 256
PEER_HALF_DIM = PEER_KEY_DIM // 2
PEER_TOKEN_BLOCK = 128
NORM_EPS = 1e-6

ATTN_Q_W = ATTN_Q_HEADS * HEAD_DIM
ATTN_KV_W = ATTN_KV_HEADS * HEAD_DIM
RET_QK_W = RET_HEADS * RET_QK_DIM
RET_V_W = RET_HEADS * RET_V_DIM
IN_WIDTHS = (ATTN_Q_W, ATTN_KV_W, ATTN_KV_W, RET_QK_W, RET_QK_W, RET_V_W, RET_V_W, D_MODEL, D_MODEL)
IN_WIDTH = ATTN_Q_W + 2 * ATTN_KV_W + 2 * RET_QK_W + 2 * RET_V_W + 2 * D_MODEL

kernel_name = "hybrid_swa_retention_peer_block"


def _split_points():
    pts, acc = [], 0
    for w in IN_WIDTHS[:-1]:
        acc += w
        pts.append(acc)
    return pts


def rmsnorm(x, g):
    xf = x.astype(jnp.float32)
    y = xf * lax.rsqrt(jnp.mean(xf * xf, axis=-1, keepdims=True) + NORM_EPS)
    return (y * g.astype(jnp.float32)).astype(x.dtype)


def rope_tables(seq):
    pos = jnp.arange(seq, dtype=jnp.float32)
    inv_freq = 1.0 / (ROPE_THETA ** (jnp.arange(0, HEAD_DIM, 2, dtype=jnp.float32) / HEAD_DIM))
    ang = pos[:, None] * inv_freq[None, :]
    return jnp.cos(ang)[:, None, :], jnp.sin(ang)[:, None, :]


def apply_rope(x, cos, sin):
    cos = cos.astype(x.dtype)
    sin = sin.astype(x.dtype)
    x1, x2 = jnp.split(x, 2, axis=-1)
    return jnp.concatenate([x1 * cos - x2 * sin, x2 * cos + x1 * sin], axis=-1)


def sliding_window_attention(q, k, v, sinks, cos, sin):
    B, S, _ = q.shape
    C = ATTN_BLOCK
    N = S // C
    q = apply_rope(q.reshape(B, S, ATTN_Q_HEADS, HEAD_DIM), cos, sin)
    k = apply_rope(k.reshape(B, S, ATTN_KV_HEADS, HEAD_DIM), cos, sin)
    v = v.reshape(B, S, ATTN_KV_HEADS, HEAD_DIM)
    qb = q.reshape(B, N, C, ATTN_KV_HEADS, ATTN_GROUP, HEAD_DIM)

    def banded(t):
        tb = t.reshape(B, N, C, ATTN_KV_HEADS, HEAD_DIM)
        prev = jnp.concatenate([jnp.zeros_like(tb[:, :1]), tb[:, :-1]], axis=1)
        return jnp.concatenate([prev, tb], axis=2)

    kb, vb = banded(k), banded(v)
    scores = jnp.einsum('bnqhgd,bnkhd->bhgnqk', qb, kb).astype(jnp.float32) * (HEAD_DIM ** -0.5)
    qi = jnp.arange(C)[:, None]
    kj = jnp.arange(2 * C)[None, :]
    diff = qi + C - kj
    blk = jnp.arange(N)[:, None, None]
    allowed = (diff >= 0) & (diff < WINDOW) & ((blk > 0) | (kj >= C))
    scores = jnp.where(allowed, scores, jnp.float32(-1e30))
    sink = sinks.astype(jnp.float32).reshape(ATTN_KV_HEADS, ATTN_GROUP)[None, :, :, None, None, None]
    sink = jnp.broadcast_to(sink, scores.shape[:-1] + (1,))
    probs = jax.nn.softmax(jnp.concatenate([scores, sink], axis=-1), axis=-1)[..., :-1]
    out = jnp.einsum('bhgnqk,bnkhd->bnqhgd', probs.astype(v.dtype), vb)
    return out.reshape(B, S, ATTN_Q_W)


def retention(q, k, v, g, cos, sin):
    B, S, _ = q.shape
    C = RET_CHUNK
    N = S // C
    H = RET_HEADS
    q = apply_rope(q.reshape(B, S, H, RET_QK_DIM), cos, sin)
    k = apply_rope(k.reshape(B, S, H, RET_QK_DIM), cos, sin) * (RET_QK_DIM ** -0.5)
    v = v.reshape(B, S, H, RET_V_DIM)
    qc = q.reshape(B, N, C, H, RET_QK_DIM).transpose(0, 3, 1, 2, 4).astype(jnp.float32)
    kc = k.reshape(B, N, C, H, RET_QK_DIM).transpose(0, 3, 1, 2, 4).astype(jnp.float32)
    vc = v.reshape(B, N, C, H, RET_V_DIM).transpose(0, 3, 1, 2, 4).astype(jnp.float32)

    log_gamma = jnp.log(1.0 - 2.0 ** (-5.0 - jnp.arange(H, dtype=jnp.float32)))
    pos = jnp.arange(C, dtype=jnp.float32)
    d = pos[:, None] - pos[None, :]
    decay_mask = jnp.where(d >= 0, jnp.exp(log_gamma[:, None, None] * jnp.maximum(d, 0.0)), 0.0)
    inner = jnp.einsum('bhncd,bhnkd->bhnck', qc, kc) * decay_mask[None, :, None]
    o_intra = jnp.einsum('bhnck,bhnke->bhnce', inner, vc)

    k_decay = jnp.exp(log_gamma[:, None] * (C - 1.0 - pos)[None, :])
    q_decay = jnp.exp(log_gamma[:, None] * (pos + 1.0)[None, :])
    chunk_decay = jnp.exp(log_gamma * C)
    kv = jnp.einsum('bhnkd,bhnke->bhnde', kc * k_decay[None, :, None, :, None], vc)

    def step(state, kv_n):
        return state * chunk_decay[None, :, None, None] + kv_n, state

    init = jnp.zeros((B, H, RET_QK_DIM, RET_V_DIM), jnp.float32)
    _, prev_states = lax.scan(step, init, kv.transpose(2, 0, 1, 3, 4))
    prev_states = prev_states.transpose(1, 2, 0, 3, 4)
    o_cross = jnp.einsum('bhncd,bhnde->bhnce', qc * q_decay[None, :, None, :, None], prev_states)
    o = (o_intra + o_cross).transpose(0, 2, 3, 1, 4).reshape(B, S, H, RET_V_DIM)
    mu = jnp.mean(o, axis=-1, keepdims=True)
    var = jnp.mean(jnp.square(o - mu), axis=-1, keepdims=True)
    o = ((o - mu) * lax.rsqrt(var + NORM_EPS)).astype(g.dtype).reshape(B, S, RET_V_W)
    return o * jax.nn.silu(g)


def peer(x, w_query, sub_keys, expert_down, expert_up):
    B, S, D = x.shape
    T = B * S
    H, K = PEER_HEADS, PEER_TOPK
    xt = x.reshape(T, D)
    qry = (xt @ w_query).reshape(T, H, 2, PEER_HALF_DIM)
    s = jnp.einsum('thpd,phnd->thpn', qry, sub_keys).astype(jnp.float32)
    top_s, top_i = lax.top_k(s, K)
    cand_s = top_s[:, :, 0, :, None] + top_s[:, :, 1, None, :]
    cand_i = top_i[:, :, 0, :, None] * PEER_N_KEYS + top_i[:, :, 1, None, :]
    best_s, best_pos = lax.top_k(cand_s.reshape(T, H, K * K), K)
    expert_idx = jnp.take_along_axis(cand_i.reshape(T, H, K * K), best_pos, axis=-1)
    gates = jax.nn.softmax(best_s, axis=-1).astype(x.dtype)
    nb = T // PEER_TOKEN_BLOCK

    def block(args):
        xb, ib, gb = args
        u = expert_down[ib]
        act = jax.nn.gelu(jnp.einsum('td,thkd->thk', xb, u), approximate=False)
        vv = expert_up[ib]
        return jnp.einsum('thk,thkd->td', gb * act, vv)

    y = lax.map(block, (xt.reshape(nb, PEER_TOKEN_BLOCK, D),
                        expert_idx.reshape(nb, PEER_TOKEN_BLOCK, H, K),
                        gates.reshape(nb, PEER_TOKEN_BLOCK, H, K)))
    return y.reshape(B, S, D)


def setup_inputs(seed: int = 0) -> dict:
    key = jax.random.key(seed)
    ks = jax.random.split(key, 14)
    f32 = jnp.float32
    L, D = DEPTH, D_MODEL
    nrm = lambda k, shape, scale: jax.random.normal(k, shape, f32) * scale
    return {
        "x": jax.random.normal(ks[0], (BATCH, SEQ, D), f32),
        "attn_norm": 1.0 + nrm(ks[1], (L, D), 0.02),
        "w_in": nrm(ks[2], (L, D, IN_WIDTH), D ** -0.5),
        "attn_sinks": nrm(ks[3], (L, ATTN_Q_HEADS), 0.5),
        "w_attn_branch": nrm(ks[4], (L, ATTN_Q_W, D), ATTN_Q_W ** -0.5),
        "w_ret_branch": nrm(ks[5], (L, RET_V_W, D), RET_V_W ** -0.5),
        "w_out": nrm(ks[6], (L, D, D), D ** -0.5),
        "ffn_norm": 1.0 + nrm(ks[7], (L, D), 0.02),
        "w_peer_query": nrm(ks[8], (L, D, PEER_HEADS * PEER_KEY_DIM), D ** -0.5),
        "peer_sub_keys": nrm(ks[9], (L, 2, PEER_HEADS, PEER_N_KEYS, PEER_HALF_DIM), PEER_HALF_DIM ** -0.5),
        "peer_expert_down": nrm(ks[10], (L, PEER_N_EXPERTS, D), D ** -0.5),
        "peer_expert_up": nrm(ks[11], (L, PEER_N_EXPERTS, D), PEER_HEADS ** -0.5),
        "final_norm": 1.0 + nrm(ks[12], (D,), 0.02),
    }


def reference(x, attn_norm, w_in, attn_sinks, w_attn_branch, w_ret_branch, w_out, ffn_norm,
              w_peer_query, peer_sub_keys, peer_expert_down, peer_expert_up, final_norm):
    S = x.shape[1]
    cos, sin = rope_tables(S)
    splits = _split_points()
    h = x
    for layer in range(DEPTH):
        xn = rmsnorm(h, attn_norm[layer])
        proj = xn @ w_in[layer]
        q_a, k_a, v_a, q_r, k_r, v_r, g_r, gate_a, gate_r = jnp.split(proj, splits, axis=-1)
        y_a = sliding_window_attention(q_a, k_a, v_a, attn_sinks[layer], cos, sin) @ w_attn_branch[layer]
        y_r = retention(q_r, k_r, v_r, g_r, cos, sin) @ w_ret_branch[layer]
        merged = jax.nn.sigmoid(gate_a) * y_a + jax.nn.sigmoid(gate_r) * y_r
        h = h + merged @ w_out[layer]
        hn = rmsnorm(h, ffn_norm[layer])
        h = h + peer(hn, w_peer_query[layer], peer_sub_keys[layer],
                     peer_expert_down[layer], peer_expert_up[layer])
    return rmsnorm(h, final_norm)
```

```python
import functools

import jax
import jax.numpy as jnp
from jax import lax
from jax.experimental import pallas as pl
from jax.experimental.pallas import tpu as pltpu

F32 = jnp.float32
BF16 = jnp.bfloat16

HEAD_DIM = 128
ATTN_Q_HEADS = 8
ATTN_KV_HEADS = 2
ATTN_GROUP = ATTN_Q_HEADS // ATTN_KV_HEADS
WINDOW = 128
CHUNK = 128
ROPE_THETA = 10000.0
RET_HEADS = 8
RET_QK_DIM = 128
RET_V_DIM = 256
PEER_HEADS = 8
PEER_N_KEYS = 128
PEER_TOPK = 16
PEER_HALF_DIM = 128
NORM_EPS = 1e-6
NEG_INF = float("-inf")

LANES = 128
SUBLANES = 8
VMEM_LIMIT = 48 * 1024 * 1024

SEG_W2048 = 2048
COL_VR, COL_GR, COL_GA, COL_GRT = 0, 2048, 4096, 6144
COL_QA, COL_QR, COL_KR = 8192, 9216, 10240
COL_KA, COL_VA = 11264, 11520
IN_WIDTH = 11776


def _params(sem):
    return pltpu.CompilerParams(dimension_semantics=sem, vmem_limit_bytes=VMEM_LIMIT)


def _rope(x, cos_full, sin_signed):
    return x * cos_full + pltpu.roll(x, HEAD_DIM // 2, axis=1) * sin_signed


def _inproj_kernel(x_ref, g_ref, w_ref, o_ref, xn_ref):
    @pl.when(pl.program_id(1) == 0)
    def _():
        x = x_ref[...]
        ms = jnp.mean(x * x, axis=-1, keepdims=True)
        xn_ref[...] = (x * lax.rsqrt(ms + NORM_EPS) * g_ref[...]).astype(BF16)

    o_ref[...] = jnp.dot(xn_ref[...], w_ref[...], preferred_element_type=F32).astype(o_ref.dtype)


def _inproj(x2, gain, w_bf, tm, tn):
    T, D = x2.shape
    W = w_bf.shape[1]
    return pl.pallas_call(
        _inproj_kernel,
        grid=(T // tm, W // tn),
        in_specs=[
            pl.BlockSpec((tm, D), lambda i, j: (i, 0)),
            pl.BlockSpec((1, D), lambda i, j: (0, 0)),
            pl.BlockSpec((D, tn), lambda i, j: (0, j)),
        ],
        out_specs=pl.BlockSpec((tm, tn), lambda i, j: (i, j)),
        out_shape=jax.ShapeDtypeStruct((T, W), BF16),
        scratch_shapes=[pltpu.VMEM((tm, D), BF16)],
        compiler_params=_params(("parallel", "arbitrary")),
        name="in_proj",
    )(x2, gain, w_bf)


def _attn_kernel(sinks_ref, q_ref, kc_ref, kp_ref, vc_ref, vp_ref,
                 cos_ref, sin_ref, cosp_ref, sinp_ref, o_ref):
    n = pl.program_id(1)
    C = CHUNK
    cos, sin = cos_ref[...], sin_ref[...]
    cosp, sinp = cosp_ref[...], sinp_ref[...]
    qi = lax.broadcasted_iota(jnp.int32, (C, 2 * C), 0)
    kj = lax.broadcasted_iota(jnp.int32, (C, 2 * C), 1)
    diff = qi + C - kj
    in_band = jnp.where(diff >= 0, jnp.where(diff < WINDOW, 1, 0), 0)
    has_prev = jnp.where(n > 0, 1, 0)
    allowed = in_band * jnp.where(kj >= C, 1, has_prev) > 0

    for hk in range(ATTN_KV_HEADS):
        ksl = slice(hk * HEAD_DIM, (hk + 1) * HEAD_DIM)
        kc = _rope(kc_ref[:, ksl].astype(F32), cos, sin)
        kp = _rope(kp_ref[:, ksl].astype(F32), cosp, sinp)
        kb = jnp.concatenate([kp, kc], axis=0).astype(BF16)
        vb = jnp.concatenate([vp_ref[:, ksl], vc_ref[:, ksl]], axis=0)
        for g in range(ATTN_GROUP):
            h = hk * ATTN_GROUP + g
            hsl = slice(h * HEAD_DIM, (h + 1) * HEAD_DIM)
            q = _rope(q_ref[:, hsl].astype(F32), cos, sin).astype(BF16)
            s = lax.dot_general(q, kb, (((1,), (1,)), ((), ())), preferred_element_type=F32)
            s = jnp.where(allowed, s * (HEAD_DIM ** -0.5), -1e30)
            sink = sinks_ref[h]
            m = jnp.maximum(jnp.max(s, axis=-1, keepdims=True), sink)
            p = jnp.exp(s - m)
            denom = jnp.sum(p, axis=-1, keepdims=True) + jnp.exp(sink - m)
            o = jnp.dot(p.astype(BF16), vb, preferred_element_type=F32)
            o_ref[:, hsl] = (o / denom).astype(o_ref.dtype)


def _attention(proj, sinks, cos_full, sin_signed, B, S):
    C = CHUNK
    N = S // C
    qa_blk = COL_QA // (ATTN_Q_HEADS * HEAD_DIM)
    ka_blk = COL_KA // (ATTN_KV_HEADS * HEAD_DIM)
    va_blk = COL_VA // (ATTN_KV_HEADS * HEAD_DIM)
    kvw = ATTN_KV_HEADS * HEAD_DIM
    cur = lambda b, n: b * N + n
    prev = lambda b, n: b * N + jnp.maximum(n - 1, 0)
    return pl.pallas_call(
        _attn_kernel,
        grid=(B, N),
        in_specs=[
            pl.BlockSpec(memory_space=pltpu.SMEM),
            pl.BlockSpec((C, ATTN_Q_HEADS * HEAD_DIM), lambda b, n: (cur(b, n), qa_blk)),
            pl.BlockSpec((C, kvw), lambda b, n: (cur(b, n), ka_blk)),
            pl.BlockSpec((C, kvw), lambda b, n: (prev(b, n), ka_blk)),
            pl.BlockSpec((C, kvw), lambda b, n: (cur(b, n), va_blk)),
            pl.BlockSpec((C, kvw), lambda b, n: (prev(b, n), va_blk)),
            pl.BlockSpec((C, HEAD_DIM), lambda b, n: (n, 0)),
            pl.BlockSpec((C, HEAD_DIM), lambda b, n: (n, 0)),
            pl.BlockSpec((C, HEAD_DIM), lambda b, n: (jnp.maximum(n - 1, 0), 0)),
            pl.BlockSpec((C, HEAD_DIM), lambda b, n: (jnp.maximum(n - 1, 0), 0)),
        ],
        out_specs=pl.BlockSpec((C, ATTN_Q_HEADS * HEAD_DIM), lambda b, n: (cur(b, n), 0)),
        out_shape=jax.ShapeDtypeStruct((B * S, ATTN_Q_HEADS * HEAD_DIM), BF16),
        compiler_params=_params(("parallel", "arbitrary")),
        name="swa_attention",
    )(sinks, proj, proj, proj, proj, proj, cos_full, sin_signed, cos_full, sin_signed)


def _ret_kernel(cdec_ref, q_ref, k_ref, v_ref, g_ref, cos_ref, sin_ref,
                dmask_ref, qdec_ref, kdec_ref, o_ref, state_ref):
    @pl.when(pl.program_id(1) == 0)
    def _():
        state_ref[...] = jnp.zeros_like(state_ref)

    cos, sin = cos_ref[...], sin_ref[...]
    for h in range(RET_HEADS):
        qsl = slice(h * RET_QK_DIM, (h + 1) * RET_QK_DIM)
        vsl = slice(h * RET_V_DIM, (h + 1) * RET_V_DIM)
        q = _rope(q_ref[:, qsl].astype(F32), cos, sin)
        k = _rope(k_ref[:, qsl].astype(F32), cos, sin) * (RET_QK_DIM ** -0.5)
        v = v_ref[:, vsl]
        inner = lax.dot_general(q.astype(BF16), k.astype(BF16), (((1,), (1,)), ((), ())),
                                preferred_element_type=F32) * dmask_ref[h]
        o = jnp.dot(inner.astype(BF16), v, preferred_element_type=F32)
        state = state_ref[h]
        o = o + jnp.dot((q * qdec_ref[h]).astype(BF16), state.astype(BF16),
                        preferred_element_type=F32)
        kd_t = (k * kdec_ref[h]).T.astype(BF16)
        state_ref[h] = state * cdec_ref[h] + jnp.dot(kd_t, v, preferred_element_type=F32)
        mu = jnp.mean(o, axis=-1, keepdims=True)
        oc = o - mu
        var = jnp.mean(oc * oc, axis=-1, keepdims=True)
        gate = g_ref[:, vsl].astype(F32)
        o_ref[:, vsl] = (oc * lax.rsqrt(var + NORM_EPS) * (gate * jax.nn.sigmoid(gate))).astype(o_ref.dtype)


def _retention(proj, cos_full, sin_signed, B, S):
    C = CHUNK
    N = S // C
    H = RET_HEADS
    log_gamma = jnp.log(1.0 - 2.0 ** (-5.0 - jnp.arange(H, dtype=F32)))
    pos = jnp.arange(C, dtype=F32)
    d = pos[:, None] - pos[None, :]
    dmask = jnp.where(d >= 0, jnp.exp(log_gamma[:, None, None] * jnp.maximum(d, 0.0)), 0.0)
    kdec = jnp.exp(log_gamma[:, None] * (C - 1.0 - pos)[None, :])
    qdec = jnp.exp(log_gamma[:, None] * (pos + 1.0)[None, :])
    cdec = jnp.exp(log_gamma * C)
    qdec_b = jnp.broadcast_to(qdec[:, :, None], (H, C, RET_QK_DIM))
    kdec_b = jnp.broadcast_to(kdec[:, :, None], (H, C, RET_QK_DIM))
    qkw, vw = H * RET_QK_DIM, H * RET_V_DIM
    row = lambda b, n: b * N + n
    full3 = lambda b, n: (0, 0, 0)
    return pl.pallas_call(
        _ret_kernel,
        grid=(B, N),
        in_specs=[
            pl.BlockSpec(memory_space=pltpu.SMEM),
            pl.BlockSpec((C, qkw), lambda b, n: (row(b, n), COL_QR // qkw)),
            pl.BlockSpec((C, qkw), lambda b, n: (row(b, n), COL_KR // qkw)),
            pl.BlockSpec((C, vw), lambda b, n: (row(b, n), COL_VR // vw)),
            pl.BlockSpec((C, vw), lambda b, n: (row(b, n), COL_GR // vw)),
            pl.BlockSpec((C, HEAD_DIM), lambda b, n: (n, 0)),
            pl.BlockSpec((C, HEAD_DIM), lambda b, n: (n, 0)),
            pl.BlockSpec((H, C, C), full3),
            pl.BlockSpec((H, C, RET_QK_DIM), full3),
            pl.BlockSpec((H, C, RET_QK_DIM), full3),
        ],
        out_specs=pl.BlockSpec((C, vw), lambda b, n: (row(b, n), 0)),
        out_shape=jax.ShapeDtypeStruct((B * S, vw), BF16),
        scratch_shapes=[pltpu.VMEM((H, RET_QK_DIM, RET_V_DIM), F32)],
        compiler_params=_params(("parallel", "arbitrary")),
        name="retention",
    )(cdec, proj, proj, proj, proj, cos_full, sin_signed, dmask, qdec_b, kdec_b)


def _merge_kernel(ya_ref, yr_ref, wa_ref, wr_ref, ga_ref, gr_ref, o_ref):
    pa = jnp.dot(ya_ref[...], wa_ref[...], preferred_element_type=F32)
    pr = jnp.dot(yr_ref[...], wr_ref[...], preferred_element_type=F32)
    ga = jax.nn.sigmoid(ga_ref[...].astype(F32))
    gr = jax.nn.sigmoid(gr_ref[...].astype(F32))
    o_ref[...] = (ga * pa + gr * pr).astype(o_ref.dtype)


def _merge(ya, yr, wa_bf, wr_bf, proj, tm, tn):
    T = ya.shape[0]
    D = wa_bf.shape[1]
    return pl.pallas_call(
        _merge_kernel,
        grid=(T // tm, D // tn),
        in_specs=[
            pl.BlockSpec((tm, ya.shape[1]), lambda i, j: (i, 0)),
            pl.BlockSpec((tm, yr.shape[1]), lambda i, j: (i, 0)),
            pl.BlockSpec((wa_bf.shape[0], tn), lambda i, j: (0, j)),
            pl.BlockSpec((wr_bf.shape[0], tn), lambda i, j: (0, j)),
            pl.BlockSpec((tm, tn), lambda i, j: (i, COL_GA // tn + j)),
            pl.BlockSpec((tm, tn), lambda i, j: (i, COL_GRT // tn + j)),
        ],
        out_specs=pl.BlockSpec((tm, tn), lambda i, j: (i, j)),
        out_shape=jax.ShapeDtypeStruct((T, D), BF16),
        compiler_params=_params(("parallel", "arbitrary")),
        name="merge_branches",
    )(ya, yr, wa_bf, wr_bf, proj, proj)


def _outproj_kernel(x_ref, m_ref, w_ref, g_ref, h_ref, hnt_ref):
    h = x_ref[...] + jnp.dot(m_ref[...], w_ref[...], preferred_element_type=F32)
    h_ref[...] = h
    ms = jnp.mean(h * h, axis=-1, keepdims=True)
    hn = h * lax.rsqrt(ms + NORM_EPS) * g_ref[...]
    hnt_ref[...] = hn.T.astype(hnt_ref.dtype)


def _outproj(x2, merged, w_bf, gain, tm):
    T, D = x2.shape
    return pl.pallas_call(
        _outproj_kernel,
        grid=(T // tm,),
        in_specs=[
            pl.BlockSpec((tm, D), lambda i: (i, 0)),
            pl.BlockSpec((tm, D), lambda i: (i, 0)),
            pl.BlockSpec((D, D), lambda i: (0, 0)),
            pl.BlockSpec((1, D), lambda i: (0, 0)),
        ],
        out_specs=[
            pl.BlockSpec((tm, D), lambda i: (i, 0)),
            pl.BlockSpec((D, tm), lambda i: (0, i)),
        ],
        out_shape=[
            jax.ShapeDtypeStruct((T, D), F32),
            jax.ShapeDtypeStruct((D, T), BF16),
        ],
        compiler_params=_params(("parallel",)),
        name="out_proj",
    )(x2, merged, w_bf, gain)


def _staircase():
    K = PEER_TOPK
    return [(k1, k2) for k1 in range(K) for k2 in range(K) if (k1 + 1) * (k2 + 1) <= K + 1]


def _extract_top16(s_ref, t_ref, tb):
    H, NK = PEER_HEADS, PEER_N_KEYS

    def round_body(k, idx_prev):
        m = jnp.full((H, tb), NEG_INF, F32)
        for n in range(NK):
            rows = slice(n * H, (n + 1) * H)
            v = jnp.where(idx_prev == n, NEG_INF, s_ref[rows, :])
            s_ref[rows, :] = v
            m = jnp.maximum(m, v)
        idx = jnp.full((H, tb), NK, jnp.int32)
        for n in range(NK):
            idx = jnp.minimum(idx, jnp.where(s_ref[n * H:(n + 1) * H, :] == m, n, NK))
        t_ref[pl.ds(pl.multiple_of(k * H, H), H), :] = m
        return idx

    idx_last = lax.fori_loop(0, PEER_TOPK, round_body, jnp.full((H, tb), -1, jnp.int32))
    for n in range(NK):
        rows = slice(n * H, (n + 1) * H)
        s_ref[rows, :] = jnp.where(idx_last == n, NEG_INF, s_ref[rows, :])


def _routing_kernel(hnt_ref, wq_ref, kbig_ref, a_ref, b_ref, th_ref,
                    s1o, s1w, s2o, s2w, t1, t2, bscr):
    H, NK, K = PEER_HEADS, PEER_N_KEYS, PEER_TOPK
    tb = hnt_ref.shape[1]
    half = H * PEER_HALF_DIM
    qt = jnp.dot(wq_ref[...], hnt_ref[...], preferred_element_type=F32).astype(BF16)
    sc1 = jnp.dot(kbig_ref[0], qt[:half], preferred_element_type=F32)
    sc2 = jnp.dot(kbig_ref[1], qt[half:], preferred_element_type=F32)
    s1o[...] = sc1
    s1w[...] = sc1
    s2o[...] = sc2
    s2w[...] = sc2
    _extract_top16(s1w, t1, tb)
    _extract_top16(s2w, t2, tb)

    pairs = _staircase()
    for lt in range(tb // LANES):
        ls = slice(lt * LANES, (lt + 1) * LANES)
        t1v = [t1[k * H:(k + 1) * H, ls] for k in range(K)]
        t2v = [t2[k * H:(k + 1) * H, ls] for k in range(K)]
        cand = [t1v[k1] + t2v[k2] for (k1, k2) in pairs]
        cmax = cand[0]
        zsum = jnp.zeros((H, LANES), F32)
        idx_prev = jnp.full((H, LANES), -1, jnp.int32)
        tops = []
        for r in range(K + 1):
            m = jnp.full((H, LANES), NEG_INF, F32)
            for j in range(len(cand)):
                cand[j] = jnp.where(idx_prev == j, NEG_INF, cand[j])
                m = jnp.maximum(m, cand[j])
            idx = jnp.full((H, LANES), len(cand), jnp.int32)
            for j in range(len(cand)):
                idx = jnp.minimum(idx, jnp.where(cand[j] == m, j, len(cand)))
            idx_prev = idx
            tops.append(m)
            if r < K:
                zsum = zsum + jnp.exp(m - cmax)
        inv_z = 1.0 / zsum
        th_ref[:, ls] = jnp.exp(0.5 * (tops[K - 1] + tops[K]) - cmax) * inv_z
        m1, m2 = t1v[0], t2v[0]
        for n in range(NK):
            rows = slice(n * H, (n + 1) * H)
            a_ref[rows, ls] = jnp.where(s1w[rows, ls] == NEG_INF,
                                        jnp.exp(s1o[rows, ls] - m1) * inv_z, 0.0)
            bscr[lt, rows, :] = jnp.where(s2w[rows, ls] == NEG_INF,
                                          jnp.exp(s2o[rows, ls] - m2), 0.0)
        for h in range(H):
            b_ref[h * NK:(h + 1) * NK, ls] = bscr[lt, pl.ds(h, NK, stride=H), :]


def _routing(hnt, wq_t, kbig, tb):
    D, T = hnt.shape
    R = PEER_HEADS * PEER_N_KEYS
    scr = lambda rows: pltpu.VMEM((rows, tb), F32)
    return pl.pallas_call(
        _routing_kernel,
        grid=(T // tb,),
        in_specs=[
            pl.BlockSpec((D, tb), lambda i: (0, i)),
            pl.BlockSpec(wq_t.shape, lambda i: (0, 0)),
            pl.BlockSpec(kbig.shape, lambda i: (0, 0, 0)),
        ],
        out_specs=[
            pl.BlockSpec((R, tb), lambda i: (0, i)),
            pl.BlockSpec((R, tb), lambda i: (0, i)),
            pl.BlockSpec((PEER_HEADS, tb), lambda i: (0, i)),
        ],
        out_shape=[
            jax.ShapeDtypeStruct((R, T), F32),
            jax.ShapeDtypeStruct((R, T), F32),
            jax.ShapeDtypeStruct((PEER_HEADS, T), F32),
        ],
        scratch_shapes=[scr(R), scr(R), scr(R), scr(R),
                        scr(PEER_TOPK * PEER_HEADS), scr(PEER_TOPK * PEER_HEADS),
                        pltpu.VMEM((tb // LANES, R, LANES), F32)],
        compiler_params=_params(("parallel",)),
        name="peer_routing",
    )(hnt, wq_t, kbig)


def _gelu(x):
    return 0.5 * x * (1.0 + lax.erf(x * (2.0 ** -0.5)))


def _experts_kernel(hnt_ref, down_ref, upt_ref, a_ref, b_ref, th_ref, yt_ref, at_scr, p_scr, *, rg):
    H, NK = PEER_HEADS, PEER_N_KEYS
    ec = down_ref.shape[0]

    @pl.when(pl.program_id(1) == 0)
    def _():
        yt_ref[...] = jnp.zeros_like(yt_ref)

    at_scr[...] = jnp.dot(down_ref[...], hnt_ref[...], preferred_element_type=F32)
    for c in range(ec // NK):
        def body(i, carry):
            r0 = pl.multiple_of(i * rg, rg)
            w = jnp.zeros((rg, hnt_ref.shape[1]), F32)
            for h in range(H):
                prod = a_ref[c * H + h:c * H + h + 1, :] * b_ref[pl.ds(h * NK + r0, rg), :]
                w = w + jnp.where(prod >= th_ref[h:h + 1, :], prod, 0.0)
            rows = pl.ds(c * NK + r0, rg)
            p_scr[rows, :] = (w * _gelu(at_scr[rows, :])).astype(BF16)
            return carry

        lax.fori_loop(0, NK // rg, body, 0)
    yt_ref[...] += jnp.dot(upt_ref[...], p_scr[...], preferred_element_type=F32)


def _experts(hnt, down_bf, upt_bf, a_t, b_t, th_t, tb, ec, rg=16):
    D, T = hnt.shape
    NE = down_bf.shape[0]
    a_rows = ec // PEER_N_KEYS * PEER_HEADS
    return pl.pallas_call(
        functools.partial(_experts_kernel, rg=rg),
        grid=(T // tb, NE // ec),
        in_specs=[
            pl.BlockSpec((D, tb), lambda t, e: (0, t)),
            pl.BlockSpec((ec, D), lambda t, e: (e, 0)),
            pl.BlockSpec((D, ec), lambda t, e: (0, e)),
            pl.BlockSpec((a_rows, tb), lambda t, e: (e, t)),
            pl.BlockSpec((b_t.shape[0], tb), lambda t, e: (0, t)),
            pl.BlockSpec((PEER_HEADS, tb), lambda t, e: (0, t)),
        ],
        out_specs=pl.BlockSpec((D, tb), lambda t, e: (0, t)),
        out_shape=jax.ShapeDtypeStruct((D, T), F32),
        scratch_shapes=[pltpu.VMEM((ec, tb), F32), pltpu.VMEM((ec, tb), BF16)],
        compiler_params=_params(("parallel", "arbitrary")),
        name="peer_experts",
    )(hnt, down_bf, upt_bf, a_t, b_t, th_t)


def _finish_kernel(h_ref, yt_ref, g_ref, o_ref):
    h = h_ref[...] + yt_ref[...].T
    ms = jnp.mean(h * h, axis=-1, keepdims=True)
    o_ref[...] = h * lax.rsqrt(ms + NORM_EPS) * g_ref[...]


def _finish(h, yt, gain, tm):
    T, D = h.shape
    return pl.pallas_call(
        _finish_kernel,
        grid=(T // tm,),
        in_specs=[
            pl.BlockSpec((tm, D), lambda i: (i, 0)),
            pl.BlockSpec((D, tm), lambda i: (0, i)),
            pl.BlockSpec((1, D), lambda i: (0, 0)),
        ],
        out_specs=pl.BlockSpec((tm, D), lambda i: (i, 0)),
        out_shape=jax.ShapeDtypeStruct((T, D), F32),
        compiler_params=_params(("parallel",)),
        name="finish",
    )(h, yt, gain)


def _permute_w_in(w):
    widths = (1024, 256, 256, 1024, 1024, 2048, 2048, 2048, 2048)
    offs = [0]
    for wd in widths:
        offs.append(offs[-1] + wd)
    seg = lambda i: w[:, offs[i]:offs[i + 1]]
    qa, ka, va, qr, kr, vr, gr, ga, grt = (seg(i) for i in range(9))
    return jnp.concatenate([vr, gr, ga, grt, qa, qr, kr, ka, va], axis=1).astype(BF16)


def _rope_tables(S):
    pos = jnp.arange(S, dtype=F32)
    inv_freq = 1.0 / (ROPE_THETA ** (jnp.arange(0, HEAD_DIM, 2, dtype=F32) / HEAD_DIM))
    ang = pos[:, None] * inv_freq[None, :]
    cos, sin = jnp.cos(ang), jnp.sin(ang)
    return jnp.concatenate([cos, cos], axis=1), jnp.concatenate([-sin, sin], axis=1)


def kernel(x, attn_norm, w_in, attn_sinks, w_attn_branch, w_ret_branch, w_out, ffn_norm,
           w_peer_query, peer_sub_keys, peer_expert_down, peer_expert_up, final_norm):
    B, S, D = x.shape
    T = B * S
    assert w_in.shape[0] == 1, "single-layer block"
    assert S % CHUNK == 0 and T % 512 == 0

    x2 = x.reshape(T, D)
    cos_full, sin_signed = _rope_tables(S)

    proj = _inproj(x2, attn_norm[0][None, :], _permute_w_in(w_in[0]), tm=512, tn=512)
    y_a = _attention(proj, attn_sinks[0], cos_full, sin_signed, B, S)
    y_r = _retention(proj, cos_full, sin_signed, B, S)
    merged = _merge(y_a, y_r, w_attn_branch[0].astype(BF16), w_ret_branch[0].astype(BF16),
                    proj, tm=512, tn=512)
    h, hnt = _outproj(x2, merged, w_out[0].astype(BF16), ffn_norm[0][None, :], tm=256)

    H, NK, HD = PEER_HEADS, PEER_N_KEYS, PEER_HALF_DIM
    wq_t = w_peer_query[0].reshape(D, H, 2, HD).transpose(2, 1, 3, 0).reshape(2 * H * HD, D).astype(BF16)
    eye = jnp.eye(H, dtype=F32)
    kbig = jnp.einsum("phnd,hg->pnhgd", peer_sub_keys[0], eye).reshape(2, NK * H, H * HD).astype(BF16)
    a_t, b_t, th_t = _routing(hnt, wq_t, kbig, tb=256)

    yt = _experts(hnt, peer_expert_down[0].astype(BF16), peer_expert_up[0].T.astype(BF16),
                  a_t, b_t, th_t, tb=512, ec=512)
    out = _finish(h, yt, final_norm[None, :], tm=256)
    return out.reshape(B, S, D)
```

```python
import functools

import jax
import jax.numpy as jnp
from jax import lax
from jax.experimental import pallas as pl
from jax.experimental.pallas import tpu as pltpu

F32 = jnp.float32
BF16 = jnp.bfloat16

HEAD_DIM = 128
ATTN_Q_HEADS = 8
ATTN_KV_HEADS = 2
ATTN_GROUP = ATTN_Q_HEADS // ATTN_KV_HEADS
WINDOW = 128
CHUNK = 128
ROPE_THETA = 10000.0
RET_HEADS = 8
RET_QK_DIM = 128
RET_V_DIM = 256
PEER_HEADS = 8
PEER_N_KEYS = 128
PEER_TOPK = 16
PEER_HALF_DIM = 128
NORM_EPS = 1e-6
NEG_INF = float("-inf")

LANES = 128
SUBLANES = 8
VMEM_LIMIT = 48 * 1024 * 1024

SEG_W2048 = 2048
COL_VR, COL_GR, COL_GA, COL_GRT = 0, 2048, 4096, 6144
COL_QA, COL_QR, COL_KR = 8192, 9216, 10240
COL_KA, COL_VA = 11264, 11520
IN_WIDTH = 11776


def _params(sem):
    return pltpu.CompilerParams(dimension_semantics=sem, vmem_limit_bytes=VMEM_LIMIT)


def _rope(x, cos_full, sin_signed):
    return x * cos_full + pltpu.roll(x, HEAD_DIM // 2, axis=1) * sin_signed


def _inproj_kernel(x_ref, g_ref, w_ref, o_ref, xn_ref):
    @pl.when(pl.program_id(1) == 0)
    def _():
        x = x_ref[...]
        ms = jnp.mean(x * x, axis=-1, keepdims=True)
        xn_ref[...] = (x * lax.rsqrt(ms + NORM_EPS) * g_ref[...]).astype(BF16)

    o_ref[...] = jnp.dot(xn_ref[...], w_ref[...], preferred_element_type=F32).astype(o_ref.dtype)


def _inproj(x2, gain, w_bf, tm, tn):
    T, D = x2.shape
    W = w_bf.shape[1]
    return pl.pallas_call(
        _inproj_kernel,
        grid=(T // tm, W // tn),
        in_specs=[
            pl.BlockSpec((tm, D), lambda i, j: (i, 0)),
            pl.BlockSpec((1, D), lambda i, j: (0, 0)),
            pl.BlockSpec((D, tn), lambda i, j: (0, j)),
        ],
        out_specs=pl.BlockSpec((tm, tn), lambda i, j: (i, j)),
        out_shape=jax.ShapeDtypeStruct((T, W), BF16),
        scratch_shapes=[pltpu.VMEM((tm, D), BF16)],
        compiler_params=_params(("parallel", "arbitrary")),
        name="in_proj",
    )(x2, gain, w_bf)


def _attn_kernel(sinks_ref, q_ref, kc_ref, kp_ref, vc_ref, vp_ref,
                 cos_ref, sin_ref, cosp_ref, sinp_ref, o_ref):
    n = pl.program_id(1)
    C = CHUNK
    cos, sin = cos_ref[...], sin_ref[...]
    cosp, sinp = cosp_ref[...], sinp_ref[...]
    qi = lax.broadcasted_iota(jnp.int32, (C, 2 * C), 0)
    kj = lax.broadcasted_iota(jnp.int32, (C, 2 * C), 1)
    diff = qi + C - kj
    in_band = jnp.where(diff >= 0, jnp.where(diff < WINDOW, 1, 0), 0)
    has_prev = jnp.where(n > 0, 1, 0)
    allowed = in_band * jnp.where(kj >= C, 1, has_prev) > 0

    for hk in range(ATTN_KV_HEADS):
        ksl = slice(hk * HEAD_DIM, (hk + 1) * HEAD_DIM)
        kc = _rope(kc_ref[:, ksl].astype(F32), cos, sin)
        kp = _rope(kp_ref[:, ksl].astype(F32), cosp, sinp)
        kb = jnp.concatenate([kp, kc], axis=0).astype(BF16)
        vb = jnp.concatenate([vp_ref[:, ksl], vc_ref[:, ksl]], axis=0)
        for g in range(ATTN_GROUP):
            h = hk * ATTN_GROUP + g
            hsl = slice(h * HEAD_DIM, (h + 1) * HEAD_DIM)
            q = _rope(q_ref[:, hsl].astype(F32), cos, sin).astype(BF16)
            s = lax.dot_general(q, kb, (((1,), (1,)), ((), ())), preferred_element_type=F32)
            s = jnp.where(allowed, s * (HEAD_DIM ** -0.5), -1e30)
            sink = sinks_ref[h]
            m = jnp.maximum(jnp.max(s, axis=-1, keepdims=True), sink)
            p = jnp.exp(s - m)
            denom = jnp.sum(p, axis=-1, keepdims=True) + jnp.exp(sink - m)
            o = jnp.dot(p.astype(BF16), vb, preferred_element_type=F32)
            o_ref[:, hsl] = (o / denom).astype(o_ref.dtype)


def _attention(proj, sinks, cos_full, sin_signed, B, S):
    C = CHUNK
    N = S // C
    qa_blk = COL_QA // (ATTN_Q_HEADS * HEAD_DIM)
    ka_blk = COL_KA // (ATTN_KV_HEADS * HEAD_DIM)
    va_blk = COL_VA // (ATTN_KV_HEADS * HEAD_DIM)
    kvw = ATTN_KV_HEADS * HEAD_DIM
    cur = lambda b, n: b * N + n
    prev = lambda b, n: b * N + jnp.maximum(n - 1, 0)
    return pl.pallas_call(
        _attn_kernel,
        grid=(B, N),
        in_specs=[
            pl.BlockSpec(memory_space=pltpu.SMEM),
            pl.BlockSpec((C, ATTN_Q_HEADS * HEAD_DIM), lambda b, n: (cur(b, n), qa_blk)),
            pl.BlockSpec((C, kvw), lambda b, n: (cur(b, n), ka_blk)),
            pl.BlockSpec((C, kvw), lambda b, n: (prev(b, n), ka_blk)),
            pl.BlockSpec((C, kvw), lambda b, n: (cur(b, n), va_blk)),
            pl.BlockSpec((C, kvw), lambda b, n: (prev(b, n), va_blk)),
            pl.BlockSpec((C, HEAD_DIM), lambda b, n: (n, 0)),
            pl.BlockSpec((C, HEAD_DIM), lambda b, n: (n, 0)),
            pl.BlockSpec((C, HEAD_DIM), lambda b, n: (jnp.maximum(n - 1, 0), 0)),
            pl.BlockSpec((C, HEAD_DIM), lambda b, n: (jnp.maximum(n - 1, 0), 0)),
        ],
        out_specs=pl.BlockSpec((C, ATTN_Q_HEADS * HEAD_DIM), lambda b, n: (cur(b, n), 0)),
        out_shape=jax.ShapeDtypeStruct((B * S, ATTN_Q_HEADS * HEAD_DIM), BF16),
        compiler_params=_params(("parallel", "arbitrary")),
        name="swa_attention",
    )(sinks, proj, proj, proj, proj, proj, cos_full, sin_signed, cos_full, sin_signed)


def _ret_kernel(cdec_ref, q_ref, k_ref, v_ref, g_ref, cos_ref, sin_ref,
                dmask_ref, qdec_ref, kdec_ref, o_ref, state_ref):
    @pl.when(pl.program_id(1) == 0)
    def _():
        state_ref[...] = jnp.zeros_like(state_ref)

    cos, sin = cos_ref[...], sin_ref[...]
    for h in range(RET_HEADS):
        qsl = slice(h * RET_QK_DIM, (h + 1) * RET_QK_DIM)
        vsl = slice(h * RET_V_DIM, (h + 1) * RET_V_DIM)
        q = _rope(q_ref[:, qsl].astype(F32), cos, sin)
        k = _rope(k_ref[:, qsl].astype(F32), cos, sin) * (RET_QK_DIM ** -0.5)
        v = v_ref[:, vsl]
        inner = lax.dot_general(q.astype(BF16), k.astype(BF16), (((1,), (1,)), ((), ())),
                                preferred_element_type=F32) * dmask_ref[h]
        o = jnp.dot(inner.astype(BF16), v, preferred_element_type=F32)
        state = state_ref[h]
        o = o + jnp.dot((q * qdec_ref[h]).astype(BF16), state.astype(BF16),
                        preferred_element_type=F32)
        kd_t = (k * kdec_ref[h]).T.astype(BF16)
        state_ref[h] = state * cdec_ref[h] + jnp.dot(kd_t, v, preferred_element_type=F32)
        mu = jnp.mean(o, axis=-1, keepdims=True)
        oc = o - mu
        var = jnp.mean(oc * oc, axis=-1, keepdims=True)
        gate = g_ref[:, vsl].astype(F32)
        o_ref[:, vsl] = (oc * lax.rsqrt(var + NORM_EPS) * (gate * jax.nn.sigmoid(gate))).astype(o_ref.dtype)


def _retention(proj, cos_full, sin_signed, B, S):
    C = CHUNK
    N = S // C
    H = RET_HEADS
    log_gamma = jnp.log(1.0 - 2.0 ** (-5.0 - jnp.arange(H, dtype=F32)))
    pos = jnp.arange(C, dtype=F32)
    d = pos[:, None] - pos[None, :]
    dmask = jnp.where(d >= 0, jnp.exp(log_gamma[:, None, None] * jnp.maximum(d, 0.0)), 0.0)
    kdec = jnp.exp(log_gamma[:, None] * (C - 1.0 - pos)[None, :])
    qdec = jnp.exp(log_gamma[:, None] * (pos + 1.0)[None, :])
    cdec = jnp.exp(log_gamma * C)
    qdec_b = jnp.broadcast_to(qdec[:, :, None], (H, C, RET_QK_DIM))
    kdec_b = jnp.broadcast_to(kdec[:, :, None], (H, C, RET_QK_DIM))
    qkw, vw = H * RET_QK_DIM, H * RET_V_DIM
    row = lambda b, n: b * N + n
    full3 = lambda b, n: (0, 0, 0)
    return pl.pallas_call(
        _ret_kernel,
        grid=(B, N),
        in_specs=[
            pl.BlockSpec(memory_space=pltpu.SMEM),
            pl.BlockSpec((C, qkw), lambda b, n: (row(b, n), COL_QR // qkw)),
            pl.BlockSpec((C, qkw), lambda b, n: (row(b, n), COL_KR // qkw)),
            pl.BlockSpec((C, vw), lambda b, n: (row(b, n), COL_VR // vw)),
            pl.BlockSpec((C, vw), lambda b, n: (row(b, n), COL_GR // vw)),
            pl.BlockSpec((C, HEAD_DIM), lambda b, n: (n, 0)),
            pl.BlockSpec((C, HEAD_DIM), lambda b, n: (n, 0)),
            pl.BlockSpec((H, C, C), full3),
            pl.BlockSpec((H, C, RET_QK_DIM), full3),
            pl.BlockSpec((H, C, RET_QK_DIM), full3),
        ],
        out_specs=pl.BlockSpec((C, vw), lambda b, n: (row(b, n), 0)),
        out_shape=jax.ShapeDtypeStruct((B * S, vw), BF16),
        scratch_shapes=[pltpu.VMEM((H, RET_QK_DIM, RET_V_DIM), F32)],
        compiler_params=_params(("parallel", "arbitrary")),
        name="retention",
    )(cdec, proj, proj, proj, proj, cos_full, sin_signed, dmask, qdec_b, kdec_b)


def _merge_kernel(ya_ref, yr_ref, wa_ref, wr_ref, ga_ref, gr_ref, o_ref):
    pa = jnp.dot(ya_ref[...], wa_ref[...], preferred_element_type=F32)
    pr = jnp.dot(yr_ref[...], wr_ref[...], preferred_element_type=F32)
    ga = jax.nn.sigmoid(ga_ref[...].astype(F32))
    gr = jax.nn.sigmoid(gr_ref[...].astype(F32))
    o_ref[...] = (ga * pa + gr * pr).astype(o_ref.dtype)


def _merge(ya, yr, wa_bf, wr_bf, proj, tm, tn):
    T = ya.shape[0]
    D = wa_bf.shape[1]
    return pl.pallas_call(
        _merge_kernel,
        grid=(T // tm, D // tn),
        in_specs=[
            pl.BlockSpec((tm, ya.shape[1]), lambda i, j: (i, 0)),
            pl.BlockSpec((tm, yr.shape[1]), lambda i, j: (i, 0)),
            pl.BlockSpec((wa_bf.shape[0], tn), lambda i, j: (0, j)),
            pl.BlockSpec((wr_bf.shape[0], tn), lambda i, j: (0, j)),
            pl.BlockSpec((tm, tn), lambda i, j: (i, COL_GA // tn + j)),
            pl.BlockSpec((tm, tn), lambda i, j: (i, COL_GRT // tn + j)),
        ],
        out_specs=pl.BlockSpec((tm, tn), lambda i, j: (i, j)),
        out_shape=jax.ShapeDtypeStruct((T, D), BF16),
        compiler_params=_params(("parallel", "arbitrary")),
        name="merge_branches",
    )(ya, yr, wa_bf, wr_bf, proj, proj)


def _outproj_kernel(x_ref, m_ref, w_ref, g_ref, h_ref, hnt_ref):
    h = x_ref[...] + jnp.dot(m_ref[...], w_ref[...], preferred_element_type=F32)
    h_ref[...] = h
    ms = jnp.mean(h * h, axis=-1, keepdims=True)
    hn = h * lax.rsqrt(ms + NORM_EPS) * g_ref[...]
    hnt_ref[...] = hn.T.astype(hnt_ref.dtype)


def _outproj(x2, merged, w_bf, gain, tm):
    T, D = x2.shape
    return pl.pallas_call(
        _outproj_kernel,
        grid=(T // tm,),
        in_specs=[
            pl.BlockSpec((tm, D), lambda i: (i, 0)),
            pl.BlockSpec((tm, D), lambda i: (i, 0)),
            pl.BlockSpec((D, D), lambda i: (0, 0)),
            pl.BlockSpec((1, D), lambda i: (0, 0)),
        ],
        out_specs=[
            pl.BlockSpec((tm, D), lambda i: (i, 0)),
            pl.BlockSpec((D, tm), lambda i: (0, i)),
        ],
        out_shape=[
            jax.ShapeDtypeStruct((T, D), F32),
            jax.ShapeDtypeStruct((D, T), BF16),
        ],
        compiler_params=_params(("parallel",)),
        name="out_proj",
    )(x2, merged, w_bf, gain)


def _staircase():
    K = PEER_TOPK
    return [(k1, k2) for k1 in range(K) for k2 in range(K) if (k1 + 1) * (k2 + 1) <= K + 1]


def _extract_top16(s_ref, t_ref, tb):
    H, NK = PEER_HEADS, PEER_N_KEYS

    def round_body(k, idx_prev):
        m = jnp.full((H, tb), NEG_INF, F32)
        for n in range(NK):
            rows = slice(n * H, (n + 1) * H)
            v = jnp.where(idx_prev == n, NEG_INF, s_ref[rows, :])
            s_ref[rows, :] = v
            m = jnp.maximum(m, v)
        idx = jnp.full((H, tb), NK, jnp.int32)
        for n in range(NK):
            idx = jnp.minimum(idx, jnp.where(s_ref[n * H:(n + 1) * H, :] == m, n, NK))
        t_ref[pl.ds(pl.multiple_of(k * H, H), H), :] = m
        return idx

    idx_last = lax.fori_loop(0, PEER_TOPK, round_body, jnp.full((H, tb), -1, jnp.int32))
    for n in range(NK):
        rows = slice(n * H, (n + 1) * H)
        s_ref[rows, :] = jnp.where(idx_last == n, NEG_INF, s_ref[rows, :])


def _routing_kernel(hnt_ref, wq_ref, kbig_ref, a_ref, b_ref, th_ref,
                    s1o, s1w, s2o, s2w, t1, t2, bscr):
    H, NK, K = PEER_HEADS, PEER_N_KEYS, PEER_TOPK
    tb = hnt_ref.shape[1]
    half = H * PEER_HALF_DIM
    qt = jnp.dot(wq_ref[...], hnt_ref[...], preferred_element_type=F32).astype(BF16)
    sc1 = jnp.dot(kbig_ref[0], qt[:half], preferred_element_type=F32)
    sc2 = jnp.dot(kbig_ref[1], qt[half:], preferred_element_type=F32)
    s1o[...] = sc1
    s1w[...] = sc1
    s2o[...] = sc2
    s2w[...] = sc2
    _extract_top16(s1w, t1, tb)
    _extract_top16(s2w, t2, tb)

    pairs = _staircase()
    for lt in range(tb // LANES):
        ls = slice(lt * LANES, (lt + 1) * LANES)
        t1v = [t1[k * H:(k + 1) * H, ls] for k in range(K)]
        t2v = [t2[k * H:(k + 1) * H, ls] for k in range(K)]
        cand = [t1v[k1] + t2v[k2] for (k1, k2) in pairs]
        cmax = cand[0]
        zsum = jnp.zeros((H, LANES), F32)
        idx_prev = jnp.full((H, LANES), -1, jnp.int32)
        tops = []
        for r in range(K + 1):
            m = jnp.full((H, LANES), NEG_INF, F32)
            for j in range(len(cand)):
                cand[j] = jnp.where(idx_prev == j, NEG_INF, cand[j])
                m = jnp.maximum(m, cand[j])
            idx = jnp.full((H, LANES), len(cand), jnp.int32)
            for j in range(len(cand)):
                idx = jnp.minimum(idx, jnp.where(cand[j] == m, j, len(cand)))
            idx_prev = idx
            tops.append(m)
            if r < K:
                zsum = zsum + jnp.exp(m - cmax)
        inv_z = 1.0 / zsum
        th_ref[:, ls] = jnp.exp(0.5 * (tops[K - 1] + tops[K]) - cmax) * inv_z
        m1, m2 = t1v[0], t2v[0]
        for n in range(NK):
            rows = slice(n * H, (n + 1) * H)
            a_ref[rows, ls] = jnp.where(s1w[rows, ls] == NEG_INF,
                                        jnp.exp(s1o[rows, ls] - m1) * inv_z, 0.0)
            bscr[lt, rows, :] = jnp.where(s2w[rows, ls] == NEG_INF,
                                          jnp.exp(s2o[rows, ls] - m2), 0.0)
        for h in range(H):
            b_ref[h * NK:(h + 1) * NK, ls] = bscr[lt, pl.ds(h, NK, stride=H), :]


def _routing(hnt, wq_t, kbig, tb):
    D, T = hnt.shape
    R = PEER_HEADS * PEER_N_KEYS
    scr = lambda rows: pltpu.VMEM((rows, tb), F32)
    return pl.pallas_call(
        _routing_kernel,
        grid=(T // tb,),
        in_specs=[
            pl.BlockSpec((D, tb), lambda i: (0, i)),
            pl.BlockSpec(wq_t.shape, lambda i: (0, 0)),
            pl.BlockSpec(kbig.shape, lambda i: (0, 0, 0)),
        ],
        out_specs=[
            pl.BlockSpec((R, tb), lambda i: (0, i)),
            pl.BlockSpec((R, tb), lambda i: (0, i)),
            pl.BlockSpec((PEER_HEADS, tb), lambda i: (0, i)),
        ],
        out_shape=[
            jax.ShapeDtypeStruct((R, T), F32),
            jax.ShapeDtypeStruct((R, T), F32),
            jax.ShapeDtypeStruct((PEER_HEADS, T), F32),
        ],
        scratch_shapes=[scr(R), scr(R), scr(R), scr(R),
                        scr(PEER_TOPK * PEER_HEADS), scr(PEER_TOPK * PEER_HEADS),
                        pltpu.VMEM((tb // LANES, R, LANES), F32)],
        compiler_params=_params(("parallel",)),
        name="peer_routing",
    )(hnt, wq_t, kbig)


def _gelu(x):
    return 0.5 * x * (1.0 + lax.erf(x * (2.0 ** -0.5)))


def _experts_kernel(hnt_ref, down_ref, upt_ref, a_ref, b_ref, th_ref, yt_ref,
                    arep, threp, p_scr, *, sub):
    H, NK, SL = PEER_HEADS, PEER_N_KEYS, SUBLANES
    ec, tb = down_ref.shape[0], hnt_ref.shape[1]

    @pl.when(pl.program_id(1) == 0)
    def _():
        yt_ref[...] = jnp.zeros_like(yt_ref)
        for h in range(H):
            threp[h] = jnp.broadcast_to(th_ref[h:h + 1, :], (SL, tb))

    for r in range(ec // NK * H):
        arep[r] = jnp.broadcast_to(a_ref[r:r + 1, :], (SL, tb))

    hnt = hnt_ref[...]
    for s in range(ec // sub):
        at = jnp.dot(down_ref[s * sub:(s + 1) * sub, :], hnt, preferred_element_type=F32)
        for c in range(sub // NK):
            i1 = s * (sub // NK) + c
            for lt in range(tb // LANES):
                ls = slice(lt * LANES, (lt + 1) * LANES)
                for g2 in range(NK // (2 * SL)):
                    tiles = []
                    for g in (2 * g2, 2 * g2 + 1):
                        w = jnp.zeros((SL, LANES), F32)
                        for h in range(H):
                            prod = arep[i1 * H + h, :, ls] * b_ref[h * NK + g * SL:h * NK + (g + 1) * SL, ls]
                            w = w + jnp.where(prod >= threp[h, :, ls], prod, 0.0)
                        r0 = c * NK + g * SL
                        tiles.append(w * _gelu(at[r0:r0 + SL, ls]))
                    r0 = s * sub + c * NK + 2 * g2 * SL
                    p_scr[r0:r0 + 2 * SL, ls] = jnp.concatenate(tiles, axis=0).astype(BF16)
    yt_ref[...] += jnp.dot(upt_ref[...], p_scr[...], preferred_element_type=F32)


def _experts(hnt, down_bf, upt_bf, a_t, b_t, th_t, tb, ec, sub=256):
    D, T = hnt.shape
    NE = down_bf.shape[0]
    a_rows = ec // PEER_N_KEYS * PEER_HEADS
    return pl.pallas_call(
        functools.partial(_experts_kernel, sub=sub),
        grid=(T // tb, NE // ec),
        in_specs=[
            pl.BlockSpec((D, tb), lambda t, e: (0, t)),
            pl.BlockSpec((ec, D), lambda t, e: (e, 0)),
            pl.BlockSpec((D, ec), lambda t, e: (0, e)),
            pl.BlockSpec((a_rows, tb), lambda t, e: (e, t)),
            pl.BlockSpec((b_t.shape[0], tb), lambda t, e: (0, t)),
            pl.BlockSpec((PEER_HEADS, tb), lambda t, e: (0, t)),
        ],
        out_specs=pl.BlockSpec((D, tb), lambda t, e: (0, t)),
        out_shape=jax.ShapeDtypeStruct((D, T), F32),
        scratch_shapes=[pltpu.VMEM((a_rows, SUBLANES, tb), F32),
                        pltpu.VMEM((PEER_HEADS, SUBLANES, tb), F32),
                        pltpu.VMEM((ec, tb), BF16)],
        compiler_params=_params(("parallel", "arbitrary")),
        name="peer_experts",
    )(hnt, down_bf, upt_bf, a_t, b_t, th_t)


def _finish_kernel(h_ref, yt_ref, g_ref, o_ref):
    h = h_ref[...] + yt_ref[...].T
    ms = jnp.mean(h * h, axis=-1, keepdims=True)
    o_ref[...] = h * lax.rsqrt(ms + NORM_EPS) * g_ref[...]


def _finish(h, yt, gain, tm):
    T, D = h.shape
    return pl.pallas_call(
        _finish_kernel,
        grid=(T // tm,),
        in_specs=[
            pl.BlockSpec((tm, D), lambda i: (i, 0)),
            pl.BlockSpec((D, tm), lambda i: (0, i)),
            pl.BlockSpec((1, D), lambda i: (0, 0)),
        ],
        out_specs=pl.BlockSpec((tm, D), lambda i: (i, 0)),
        out_shape=jax.ShapeDtypeStruct((T, D), F32),
        compiler_params=_params(("parallel",)),
        name="finish",
    )(h, yt, gain)


def _permute_w_in(w):
    widths = (1024, 256, 256, 1024, 1024, 2048, 2048, 2048, 2048)
    offs = [0]
    for wd in widths:
        offs.append(offs[-1] + wd)
    seg = lambda i: w[:, offs[i]:offs[i + 1]]
    qa, ka, va, qr, kr, vr, gr, ga, grt = (seg(i) for i in range(9))
    return jnp.concatenate([vr, gr, ga, grt, qa, qr, kr, ka, va], axis=1).astype(BF16)


def _rope_tables(S):
    pos = jnp.arange(S, dtype=F32)
    inv_freq = 1.0 / (ROPE_THETA ** (jnp.arange(0, HEAD_DIM, 2, dtype=F32) / HEAD_DIM))
    ang = pos[:, None] * inv_freq[None, :]
    cos, sin = jnp.cos(ang), jnp.sin(ang)
    return jnp.concatenate([cos, cos], axis=1), jnp.concatenate([-sin, sin], axis=1)


def kernel(x, attn_norm, w_in, attn_sinks, w_attn_branch, w_ret_branch, w_out, ffn_norm,
           w_peer_query, peer_sub_keys, peer_expert_down, peer_expert_up, final_norm):
    B, S, D = x.shape
    T = B * S
    assert w_in.shape[0] == 1, "single-layer block"
    assert S % CHUNK == 0 and T % 512 == 0

    x2 = x.reshape(T, D)
    cos_full, sin_signed = _rope_tables(S)

    proj = _inproj(x2, attn_norm[0][None, :], _permute_w_in(w_in[0]), tm=512, tn=512)
    y_a = _attention(proj, attn_sinks[0], cos_full, sin_signed, B, S)
    y_r = _retention(proj, cos_full, sin_signed, B, S)
    merged = _merge(y_a, y_r, w_attn_branch[0].astype(BF16), w_ret_branch[0].astype(BF16),
                    proj, tm=512, tn=512)
    h, hnt = _outproj(x2, merged, w_out[0].astype(BF16), ffn_norm[0][None, :], tm=256)

    H, NK, HD = PEER_HEADS, PEER_N_KEYS, PEER_HALF_DIM
    wq_t = w_peer_query[0].reshape(D, H, 2, HD).transpose(2, 1, 3, 0).reshape(2 * H * HD, D).astype(BF16)
    eye = jnp.eye(H, dtype=F32)
    kbig = jnp.einsum("phnd,hg->pnhgd", peer_sub_keys[0], eye).reshape(2, NK * H, H * HD).astype(BF16)
    a_t, b_t, th_t = _routing(hnt, wq_t, kbig, tb=256)

    yt = _experts(hnt, peer_expert_down[0].astype(BF16), peer_expert_up[0].T.astype(BF16),
                  a_t, b_t, th_t, tb=512, ec=1024)
    out = _finish(h, yt, final_norm[None, :], tm=256)
    return out.reshape(B, S, D)
```

```python
import functools

import jax
import jax.numpy as jnp
from jax import lax
from jax.experimental import pallas as pl
from jax.experimental.pallas import tpu as pltpu

F32 = jnp.float32
BF16 = jnp.bfloat16

HEAD_DIM = 128
ATTN_Q_HEADS = 8
ATTN_KV_HEADS = 2
ATTN_GROUP = ATTN_Q_HEADS // ATTN_KV_HEADS
WINDOW = 128
CHUNK = 128
ROPE_THETA = 10000.0
RET_HEADS = 8
RET_QK_DIM = 128
RET_V_DIM = 256
PEER_HEADS = 8
PEER_N_KEYS = 128
PEER_TOPK = 16
PEER_HALF_DIM = 128
NORM_EPS = 1e-6
NEG_INF = float("-inf")

LANES = 128
SUBLANES = 8
VMEM_LIMIT = 48 * 1024 * 1024

SEG_W2048 = 2048
COL_VR, COL_GR, COL_GA, COL_GRT = 0, 2048, 4096, 6144
COL_QA, COL_QR, COL_KR = 8192, 9216, 10240
COL_KA, COL_VA = 11264, 11520
IN_WIDTH = 11776


def _params(sem):
    return pltpu.CompilerParams(dimension_semantics=sem, vmem_limit_bytes=VMEM_LIMIT)


def _rope(x, cos_full, sin_signed):
    return x * cos_full + pltpu.roll(x, HEAD_DIM // 2, axis=1) * sin_signed


def _inproj_kernel(x_ref, g_ref, w_ref, o_ref, xn_ref):
    @pl.when(pl.program_id(1) == 0)
    def _():
        x = x_ref[...]
        ms = jnp.mean(x * x, axis=-1, keepdims=True)
        xn_ref[...] = (x * lax.rsqrt(ms + NORM_EPS) * g_ref[...]).astype(BF16)

    o_ref[...] = jnp.dot(xn_ref[...], w_ref[...], preferred_element_type=F32).astype(o_ref.dtype)


def _inproj(x2, gain, w_bf, tm, tn):
    T, D = x2.shape
    W = w_bf.shape[1]
    w_blocks = w_bf.reshape(D, W // tn, tn).transpose(1, 0, 2)
    return pl.pallas_call(
        _inproj_kernel,
        grid=(T // tm, W // tn),
        in_specs=[
            pl.BlockSpec((tm, D), lambda i, j: (i, 0)),
            pl.BlockSpec((1, D), lambda i, j: (0, 0)),
            pl.BlockSpec((None, D, tn), lambda i, j: (j, 0, 0)),
        ],
        out_specs=pl.BlockSpec((tm, tn), lambda i, j: (i, j)),
        out_shape=jax.ShapeDtypeStruct((T, W), BF16),
        scratch_shapes=[pltpu.VMEM((tm, D), BF16)],
        compiler_params=_params(("parallel", "arbitrary")),
        name="in_proj",
    )(x2, gain, w_blocks)


def _attn_kernel(sinks_ref, q_ref, kc_ref, kp_ref, vc_ref, vp_ref,
                 cos_ref, sin_ref, cosp_ref, sinp_ref, o_ref):
    n = pl.program_id(1)
    C = CHUNK
    cos, sin = cos_ref[...], sin_ref[...]
    cosp, sinp = cosp_ref[...], sinp_ref[...]
    qi = lax.broadcasted_iota(jnp.int32, (C, 2 * C), 0)
    kj = lax.broadcasted_iota(jnp.int32, (C, 2 * C), 1)
    diff = qi + C - kj
    in_band = jnp.where(diff >= 0, jnp.where(diff < WINDOW, 1, 0), 0)
    has_prev = jnp.where(n > 0, 1, 0)
    allowed = in_band * jnp.where(kj >= C, 1, has_prev) > 0

    for hk in range(ATTN_KV_HEADS):
        ksl = slice(hk * HEAD_DIM, (hk + 1) * HEAD_DIM)
        kc = _rope(kc_ref[:, ksl].astype(F32), cos, sin)
        kp = _rope(kp_ref[:, ksl].astype(F32), cosp, sinp)
        kb = jnp.concatenate([kp, kc], axis=0).astype(BF16)
        vb = jnp.concatenate([vp_ref[:, ksl], vc_ref[:, ksl]], axis=0)
        for g in range(ATTN_GROUP):
            h = hk * ATTN_GROUP + g
            hsl = slice(h * HEAD_DIM, (h + 1) * HEAD_DIM)
            q = _rope(q_ref[:, hsl].astype(F32), cos, sin).astype(BF16)
            s = lax.dot_general(q, kb, (((1,), (1,)), ((), ())), preferred_element_type=F32)
            s = jnp.where(allowed, s * (HEAD_DIM ** -0.5), -1e30)
            sink = sinks_ref[h]
            m = jnp.maximum(jnp.max(s, axis=-1, keepdims=True), sink)
            p = jnp.exp(s - m)
            denom = jnp.sum(p, axis=-1, keepdims=True) + jnp.exp(sink - m)
            o = jnp.dot(p.astype(BF16), vb, preferred_element_type=F32)
            o_ref[:, hsl] = (o / denom).astype(o_ref.dtype)


def _attention(proj, sinks, cos_full, sin_signed, B, S):
    C = CHUNK
    N = S // C
    qa_blk = COL_QA // (ATTN_Q_HEADS * HEAD_DIM)
    ka_blk = COL_KA // (ATTN_KV_HEADS * HEAD_DIM)
    va_blk = COL_VA // (ATTN_KV_HEADS * HEAD_DIM)
    kvw = ATTN_KV_HEADS * HEAD_DIM
    cur = lambda b, n: b * N + n
    prev = lambda b, n: b * N + jnp.maximum(n - 1, 0)
    return pl.pallas_call(
        _attn_kernel,
        grid=(B, N),
        in_specs=[
            pl.BlockSpec(memory_space=pltpu.SMEM),
            pl.BlockSpec((C, ATTN_Q_HEADS * HEAD_DIM), lambda b, n: (cur(b, n), qa_blk)),
            pl.BlockSpec((C, kvw), lambda b, n: (cur(b, n), ka_blk)),
            pl.BlockSpec((C, kvw), lambda b, n: (prev(b, n), ka_blk)),
            pl.BlockSpec((C, kvw), lambda b, n: (cur(b, n), va_blk)),
            pl.BlockSpec((C, kvw), lambda b, n: (prev(b, n), va_blk)),
            pl.BlockSpec((C, HEAD_DIM), lambda b, n: (n, 0)),
            pl.BlockSpec((C, HEAD_DIM), lambda b, n: (n, 0)),
            pl.BlockSpec((C, HEAD_DIM), lambda b, n: (jnp.maximum(n - 1, 0), 0)),
            pl.BlockSpec((C, HEAD_DIM), lambda b, n: (jnp.maximum(n - 1, 0), 0)),
        ],
        out_specs=pl.BlockSpec((C, ATTN_Q_HEADS * HEAD_DIM), lambda b, n: (cur(b, n), 0)),
        out_shape=jax.ShapeDtypeStruct((B * S, ATTN_Q_HEADS * HEAD_DIM), BF16),
        compiler_params=_params(("parallel", "arbitrary")),
        name="swa_attention",
    )(sinks, proj, proj, proj, proj, proj, cos_full, sin_signed, cos_full, sin_signed)


def _ret_kernel(cdec_ref, q_ref, k_ref, v_ref, g_ref, cos_ref, sin_ref,
                dmask_ref, qdec_ref, kdec_ref, o_ref, state_ref):
    @pl.when(pl.program_id(1) == 0)
    def _():
        state_ref[...] = jnp.zeros_like(state_ref)

    cos, sin = cos_ref[...], sin_ref[...]
    for h in range(RET_HEADS):
        qsl = slice(h * RET_QK_DIM, (h + 1) * RET_QK_DIM)
        vsl = slice(h * RET_V_DIM, (h + 1) * RET_V_DIM)
        q = _rope(q_ref[:, qsl].astype(F32), cos, sin)
        k = _rope(k_ref[:, qsl].astype(F32), cos, sin) * (RET_QK_DIM ** -0.5)
        v = v_ref[:, vsl]
        inner = lax.dot_general(q.astype(BF16), k.astype(BF16), (((1,), (1,)), ((), ())),
                                preferred_element_type=F32) * dmask_ref[h]
        o = jnp.dot(inner.astype(BF16), v, preferred_element_type=F32)
        state = state_ref[h]
        o = o + jnp.dot((q * qdec_ref[h]).astype(BF16), state.astype(BF16),
                        preferred_element_type=F32)
        kd_t = (k * kdec_ref[h]).T.astype(BF16)
        state_ref[h] = state * cdec_ref[h] + jnp.dot(kd_t, v, preferred_element_type=F32)
        mu = jnp.mean(o, axis=-1, keepdims=True)
        oc = o - mu
        var = jnp.mean(oc * oc, axis=-1, keepdims=True)
        gate = g_ref[:, vsl].astype(F32)
        o_ref[:, vsl] = (oc * lax.rsqrt(var + NORM_EPS) * (gate * jax.nn.sigmoid(gate))).astype(o_ref.dtype)


def _retention(proj, cos_full, sin_signed, B, S):
    C = CHUNK
    N = S // C
    H = RET_HEADS
    log_gamma = jnp.log(1.0 - 2.0 ** (-5.0 - jnp.arange(H, dtype=F32)))
    pos = jnp.arange(C, dtype=F32)
    d = pos[:, None] - pos[None, :]
    dmask = jnp.where(d >= 0, jnp.exp(log_gamma[:, None, None] * jnp.maximum(d, 0.0)), 0.0)
    kdec = jnp.exp(log_gamma[:, None] * (C - 1.0 - pos)[None, :])
    qdec = jnp.exp(log_gamma[:, None] * (pos + 1.0)[None, :])
    cdec = jnp.exp(log_gamma * C)
    qdec_b = jnp.broadcast_to(qdec[:, :, None], (H, C, RET_QK_DIM))
    kdec_b = jnp.broadcast_to(kdec[:, :, None], (H, C, RET_QK_DIM))
    qkw, vw = H * RET_QK_DIM, H * RET_V_DIM
    row = lambda b, n: b * N + n
    full3 = lambda b, n: (0, 0, 0)
    return pl.pallas_call(
        _ret_kernel,
        grid=(B, N),
        in_specs=[
            pl.BlockSpec(memory_space=pltpu.SMEM),
            pl.BlockSpec((C, qkw), lambda b, n: (row(b, n), COL_QR // qkw)),
            pl.BlockSpec((C, qkw), lambda b, n: (row(b, n), COL_KR // qkw)),
            pl.BlockSpec((C, vw), lambda b, n: (row(b, n), COL_VR // vw)),
            pl.BlockSpec((C, vw), lambda b, n: (row(b, n), COL_GR // vw)),
            pl.BlockSpec((C, HEAD_DIM), lambda b, n: (n, 0)),
            pl.BlockSpec((C, HEAD_DIM), lambda b, n: (n, 0)),
            pl.BlockSpec((H, C, C), full3),
            pl.BlockSpec((H, C, RET_QK_DIM), full3),
            pl.BlockSpec((H, C, RET_QK_DIM), full3),
        ],
        out_specs=pl.BlockSpec((C, vw), lambda b, n: (row(b, n), 0)),
        out_shape=jax.ShapeDtypeStruct((B * S, vw), BF16),
        scratch_shapes=[pltpu.VMEM((H, RET_QK_DIM, RET_V_DIM), F32)],
        compiler_params=_params(("parallel", "arbitrary")),
        name="retention",
    )(cdec, proj, proj, proj, proj, cos_full, sin_signed, dmask, qdec_b, kdec_b)


def _merge_kernel(ya_ref, yr_ref, wa_ref, wr_ref, ga_ref, gr_ref, o_ref):
    pa = jnp.dot(ya_ref[...], wa_ref[...], preferred_element_type=F32)
    pr = jnp.dot(yr_ref[...], wr_ref[...], preferred_element_type=F32)
    ga = jax.nn.sigmoid(ga_ref[...].astype(F32))
    gr = jax.nn.sigmoid(gr_ref[...].astype(F32))
    o_ref[...] = (ga * pa + gr * pr).astype(o_ref.dtype)


def _merge(ya, yr, wa_bf, wr_bf, proj, tm, tn):
    T = ya.shape[0]
    D = wa_bf.shape[1]
    return pl.pallas_call(
        _merge_kernel,
        grid=(T // tm, D // tn),
        in_specs=[
            pl.BlockSpec((tm, ya.shape[1]), lambda i, j: (i, 0)),
            pl.BlockSpec((tm, yr.shape[1]), lambda i, j: (i, 0)),
            pl.BlockSpec((wa_bf.shape[0], tn), lambda i, j: (0, j)),
            pl.BlockSpec((wr_bf.shape[0], tn), lambda i, j: (0, j)),
            pl.BlockSpec((tm, tn), lambda i, j: (i, COL_GA // tn + j)),
            pl.BlockSpec((tm, tn), lambda i, j: (i, COL_GRT // tn + j)),
        ],
        out_specs=pl.BlockSpec((tm, tn), lambda i, j: (i, j)),
        out_shape=jax.ShapeDtypeStruct((T, D), BF16),
        compiler_params=_params(("parallel", "arbitrary")),
        name="merge_branches",
    )(ya, yr, wa_bf, wr_bf, proj, proj)


def _outproj_kernel(x_ref, m_ref, w_ref, g_ref, h_ref, hnt_ref):
    h = x_ref[...] + jnp.dot(m_ref[...], w_ref[...], preferred_element_type=F32)
    h_ref[...] = h
    ms = jnp.mean(h * h, axis=-1, keepdims=True)
    hn = h * lax.rsqrt(ms + NORM_EPS) * g_ref[...]
    hnt_ref[...] = hn.T.astype(hnt_ref.dtype)


def _outproj(x2, merged, w_bf, gain, tm):
    T, D = x2.shape
    return pl.pallas_call(
        _outproj_kernel,
        grid=(T // tm,),
        in_specs=[
            pl.BlockSpec((tm, D), lambda i: (i, 0)),
            pl.BlockSpec((tm, D), lambda i: (i, 0)),
            pl.BlockSpec((D, D), lambda i: (0, 0)),
            pl.BlockSpec((1, D), lambda i: (0, 0)),
        ],
        out_specs=[
            pl.BlockSpec((tm, D), lambda i: (i, 0)),
            pl.BlockSpec((D, tm), lambda i: (0, i)),
        ],
        out_shape=[
            jax.ShapeDtypeStruct((T, D), F32),
            jax.ShapeDtypeStruct((D, T), BF16),
        ],
        compiler_params=_params(("parallel",)),
        name="out_proj",
    )(x2, merged, w_bf, gain)


def _staircase():
    K = PEER_TOPK
    return [(k1, k2) for k1 in range(K) for k2 in range(K) if (k1 + 1) * (k2 + 1) <= K + 1]


def _batcher_pairs(n):
    pairs, p = [], 1
    while p < n:
        k = p
        while k >= 1:
            for j in range(k % p, n - k, 2 * k):
                for i in range(min(k, n - j - k)):
                    if (i + j) // (2 * p) == (i + j + k) // (2 * p):
                        pairs.append((i + j, i + j + k))
            k //= 2
        p *= 2
    return pairs


def _compare_exchange(v, i, j):
    v[i], v[j] = jnp.maximum(v[i], v[j]), jnp.minimum(v[i], v[j])


def _sort_desc_padded(vals):
    n = 1
    while n < len(vals):
        n *= 2
    v = list(vals) + [None] * (n - len(vals))
    for i, j in _batcher_pairs(n):
        if v[j] is None:
            continue
        if v[i] is None:
            v[i], v[j] = v[j], None
            continue
        _compare_exchange(v, i, j)
    return v


def _sorted_top16(s_ref, ls):
    H, NK, K = PEER_HEADS, PEER_N_KEYS, PEER_TOPK
    sort_net = _batcher_pairs(K)
    top = None
    for g in range(NK // K):
        v = [s_ref[(g * K + i) * H:(g * K + i + 1) * H, ls] for i in range(K)]
        for i, j in sort_net:
            _compare_exchange(v, i, j)
        if top is None:
            top = v
            continue
        top = [jnp.maximum(top[i], v[K - 1 - i]) for i in range(K)]
        k = K // 2
        while k >= 1:
            for i in range(K):
                if i & k == 0:
                    _compare_exchange(top, i, i + k)
            k //= 2
    return top


def _routing_kernel(hnt_ref, wq_ref, kbig_ref, a_ref, b_ref, th_ref, s1o, s2o, bscr):
    H, NK, K = PEER_HEADS, PEER_N_KEYS, PEER_TOPK
    tb = hnt_ref.shape[1]
    half = H * PEER_HALF_DIM
    qt = jnp.dot(wq_ref[...], hnt_ref[...], preferred_element_type=F32).astype(BF16)
    s1o[...] = jnp.dot(kbig_ref[0], qt[:half], preferred_element_type=F32)
    s2o[...] = jnp.dot(kbig_ref[1], qt[half:], preferred_element_type=F32)

    pairs = _staircase()
    for lt in range(tb // LANES):
        ls = slice(lt * LANES, (lt + 1) * LANES)
        t1v = _sorted_top16(s1o, ls)
        t2v = _sorted_top16(s2o, ls)
        cand = [t1v[k1] + t2v[k2] for (k1, k2) in pairs]
        cmax = cand[0]
        tops = _sort_desc_padded(cand)[:K + 1]
        zsum = jnp.zeros((H, LANES), F32)
        for r in range(K):
            zsum = zsum + jnp.exp(tops[r] - cmax)
        inv_z = 1.0 / zsum
        th_ref[:, ls] = jnp.exp(0.5 * (tops[K - 1] + tops[K]) - cmax) * inv_z
        m1, m2 = t1v[0], t2v[0]
        for n in range(NK):
            rows = slice(n * H, (n + 1) * H)
            s1, s2 = s1o[rows, ls], s2o[rows, ls]
            a_ref[rows, ls] = jnp.where(s1 >= t1v[K - 1], jnp.exp(s1 - m1) * inv_z, 0.0)
            bscr[lt, rows, :] = jnp.where(s2 >= t2v[K - 1], jnp.exp(s2 - m2), 0.0)
        for h in range(H):
            b_ref[h * NK:(h + 1) * NK, ls] = bscr[lt, pl.ds(h, NK, stride=H), :]


def _routing(hnt, wq_t, kbig, tb):
    D, T = hnt.shape
    R = PEER_HEADS * PEER_N_KEYS
    scr = lambda rows: pltpu.VMEM((rows, tb), F32)
    return pl.pallas_call(
        _routing_kernel,
        grid=(T // tb,),
        in_specs=[
            pl.BlockSpec((D, tb), lambda i: (0, i)),
            pl.BlockSpec(wq_t.shape, lambda i: (0, 0)),
            pl.BlockSpec(kbig.shape, lambda i: (0, 0, 0)),
        ],
        out_specs=[
            pl.BlockSpec((R, tb), lambda i: (0, i)),
            pl.BlockSpec((R, tb), lambda i: (0, i)),
            pl.BlockSpec((PEER_HEADS, tb), lambda i: (0, i)),
        ],
        out_shape=[
            jax.ShapeDtypeStruct((R, T), F32),
            jax.ShapeDtypeStruct((R, T), F32),
            jax.ShapeDtypeStruct((PEER_HEADS, T), F32),
        ],
        scratch_shapes=[scr(R), scr(R), pltpu.VMEM((tb // LANES, R, LANES), F32)],
        compiler_params=_params(("parallel",)),
        name="peer_routing",
    )(hnt, wq_t, kbig)


def _gelu(x):
    return 0.5 * x * (1.0 + lax.erf(x * (2.0 ** -0.5)))


def _experts_kernel(hnt_ref, down_ref, upt_ref, a_ref, b_ref, th_ref, yt_ref,
                    arep, threp, p_scr, *, sub, halves):
    H, NK, SL = PEER_HEADS, PEER_N_KEYS, SUBLANES
    ec, tb = down_ref.shape[0], hnt_ref.shape[1]

    @pl.when(pl.program_id(1) == 0)
    def _():
        yt_ref[...] = jnp.zeros_like(yt_ref)
        for h in range(H):
            threp[h] = jnp.broadcast_to(th_ref[h:h + 1, :], (SL, tb))

    for r in range(ec // NK * H):
        arep[r] = jnp.broadcast_to(a_ref[r:r + 1, :], (SL, tb))

    hnt = hnt_ref[...]
    n_sub = ec // sub
    for s in range(n_sub):
        at = jnp.dot(down_ref[s * sub:(s + 1) * sub, :], hnt, preferred_element_type=F32)
        for c in range(sub // NK):
            i1 = s * (sub // NK) + c
            for lt in range(tb // LANES):
                ls = slice(lt * LANES, (lt + 1) * LANES)
                for g2 in range(NK // (2 * SL)):
                    tiles = []
                    for g in (2 * g2, 2 * g2 + 1):
                        w = jnp.zeros((SL, LANES), F32)
                        for h in range(H):
                            prod = arep[i1 * H + h, :, ls] * b_ref[h * NK + g * SL:h * NK + (g + 1) * SL, ls]
                            w = w + jnp.where(prod >= threp[h, :, ls], prod, 0.0)
                        r0 = c * NK + g * SL
                        tiles.append(w * _gelu(at[r0:r0 + SL, ls]))
                    r0 = s * sub + c * NK + 2 * g2 * SL
                    p_scr[r0:r0 + 2 * SL, ls] = jnp.concatenate(tiles, axis=0).astype(BF16)
        if (s + 1) % (n_sub // halves) == 0:
            k0, k1 = (s + 1 - n_sub // halves) * sub, (s + 1) * sub
            yt_ref[...] += jnp.dot(upt_ref[:, k0:k1], p_scr[k0:k1, :], preferred_element_type=F32)


def _experts(hnt, down_bf, upt_bf, a_t, b_t, th_t, tb, ec, sub=512, halves=1):
    D, T = hnt.shape
    NE = down_bf.shape[0]
    a_rows = ec // PEER_N_KEYS * PEER_HEADS
    return pl.pallas_call(
        functools.partial(_experts_kernel, sub=sub, halves=halves),
        grid=(T // tb, NE // ec),
        in_specs=[
            pl.BlockSpec((D, tb), lambda t, e: (0, t)),
            pl.BlockSpec((ec, D), lambda t, e: (e, 0)),
            pl.BlockSpec((D, ec), lambda t, e: (0, e)),
            pl.BlockSpec((a_rows, tb), lambda t, e: (e, t)),
            pl.BlockSpec((b_t.shape[0], tb), lambda t, e: (0, t)),
            pl.BlockSpec((PEER_HEADS, tb), lambda t, e: (0, t)),
        ],
        out_specs=pl.BlockSpec((D, tb), lambda t, e: (0, t)),
        out_shape=jax.ShapeDtypeStruct((D, T), F32),
        scratch_shapes=[pltpu.VMEM((a_rows, SUBLANES, tb), F32),
                        pltpu.VMEM((PEER_HEADS, SUBLANES, tb), F32),
                        pltpu.VMEM((ec, tb), BF16)],
        compiler_params=_params(("parallel", "arbitrary")),
        name="peer_experts",
    )(hnt, down_bf, upt_bf, a_t, b_t, th_t)


def _finish_kernel(h_ref, yt_ref, g_ref, o_ref):
    h = h_ref[...] + yt_ref[...].T
    ms = jnp.mean(h * h, axis=-1, keepdims=True)
    o_ref[...] = h * lax.rsqrt(ms + NORM_EPS) * g_ref[...]


def _finish(h, yt, gain, tm):
    T, D = h.shape
    return pl.pallas_call(
        _finish_kernel,
        grid=(T // tm,),
        in_specs=[
            pl.BlockSpec((tm, D), lambda i: (i, 0)),
            pl.BlockSpec((D, tm), lambda i: (0, i)),
            pl.BlockSpec((1, D), lambda i: (0, 0)),
        ],
        out_specs=pl.BlockSpec((tm, D), lambda i: (i, 0)),
        out_shape=jax.ShapeDtypeStruct((T, D), F32),
        compiler_params=_params(("parallel",)),
        name="finish",
    )(h, yt, gain)


def _permute_w_in(w):
    widths = (1024, 256, 256, 1024, 1024, 2048, 2048, 2048, 2048)
    offs = [0]
    for wd in widths:
        offs.append(offs[-1] + wd)
    seg = lambda i: w[:, offs[i]:offs[i + 1]]
    qa, ka, va, qr, kr, vr, gr, ga, grt = (seg(i) for i in range(9))
    return jnp.concatenate([vr, gr, ga, grt, qa, qr, kr, ka, va], axis=1).astype(BF16)


def _rope_tables(S):
    pos = jnp.arange(S, dtype=F32)
    inv_freq = 1.0 / (ROPE_THETA ** (jnp.arange(0, HEAD_DIM, 2, dtype=F32) / HEAD_DIM))
    ang = pos[:, None] * inv_freq[None, :]
    cos, sin = jnp.cos(ang), jnp.sin(ang)
    return jnp.concatenate([cos, cos], axis=1), jnp.concatenate([-sin, sin], axis=1)


def kernel(x, attn_norm, w_in, attn_sinks, w_attn_branch, w_ret_branch, w_out, ffn_norm,
           w_peer_query, peer_sub_keys, peer_expert_down, peer_expert_up, final_norm):
    B, S, D = x.shape
    T = B * S
    assert w_in.shape[0] == 1, "single-layer block"
    assert S % CHUNK == 0 and T % 512 == 0

    x2 = x.reshape(T, D)
    cos_full, sin_signed = _rope_tables(S)

    proj = _inproj(x2, attn_norm[0][None, :], _permute_w_in(w_in[0]), tm=1024, tn=512)
    y_a = _attention(proj, attn_sinks[0], cos_full, sin_signed, B, S)
    y_r = _retention(proj, cos_full, sin_signed, B, S)
    merged = _merge(y_a, y_r, w_attn_branch[0].astype(BF16), w_ret_branch[0].astype(BF16),
                    proj, tm=512, tn=512)
    h, hnt = _outproj(x2, merged, w_out[0].astype(BF16), ffn_norm[0][None, :], tm=256)

    H, NK, HD = PEER_HEADS, PEER_N_KEYS, PEER_HALF_DIM
    wq_t = w_peer_query[0].reshape(D, H, 2, HD).transpose(2, 1, 3, 0).reshape(2 * H * HD, D).astype(BF16)
    eye = jnp.eye(H, dtype=F32)
    kbig = jnp.einsum("phnd,hg->pnhgd", peer_sub_keys[0], eye).reshape(2, NK * H, H * HD).astype(BF16)
    a_t, b_t, th_t = _routing(hnt, wq_t, kbig, tb=256)

    yt = _experts(hnt, peer_expert_down[0].astype(BF16), peer_expert_up[0].T.astype(BF16),
                  a_t, b_t, th_t, tb=512, ec=1024)
    out = _finish(h, yt, final_norm[None, :], tm=256)
    return out.reshape(B, S, D)
```

```python
import functools

import jax
import jax.numpy as jnp
from jax import lax
from jax.experimental import pallas as pl
from jax.experimental.pallas import tpu as pltpu

F32 = jnp.float32
BF16 = jnp.bfloat16

HEAD_DIM = 128
ATTN_Q_HEADS = 8
ATTN_KV_HEADS = 2
ATTN_GROUP = ATTN_Q_HEADS // ATTN_KV_HEADS
WINDOW = 128
CHUNK = 128
ROPE_THETA = 10000.0
RET_HEADS = 8
RET_QK_DIM = 128
RET_V_DIM = 256
PEER_HEADS = 8
PEER_N_KEYS = 128
PEER_TOPK = 16
PEER_HALF_DIM = 128
NORM_EPS = 1e-6
NEG_INF = float("-inf")

LANES = 128
SUBLANES = 8
VMEM_LIMIT = 48 * 1024 * 1024

SEG_W2048 = 2048
COL_VR, COL_GR, COL_GA, COL_GRT = 0, 2048, 4096, 6144
COL_QA, COL_QR, COL_KR = 8192, 9216, 10240
COL_KA, COL_VA = 11264, 11520
IN_WIDTH = 11776


def _params(sem):
    return pltpu.CompilerParams(dimension_semantics=sem, vmem_limit_bytes=VMEM_LIMIT)


def _rope(x, cos_full, sin_signed):
    return x * cos_full + pltpu.roll(x, HEAD_DIM // 2, axis=1) * sin_signed


def _inproj_kernel(x_ref, g_ref, w_ref, o_ref, xn_ref):
    @pl.when(pl.program_id(1) == 0)
    def _():
        x = x_ref[...]
        ms = jnp.mean(x * x, axis=-1, keepdims=True)
        xn_ref[...] = (x * lax.rsqrt(ms + NORM_EPS) * g_ref[...]).astype(BF16)

    o_ref[...] = jnp.dot(xn_ref[...], w_ref[...], preferred_element_type=F32).astype(o_ref.dtype)


def _inproj(x2, gain, w_blocks, tm):
    T, D = x2.shape
    nb, _, tn = w_blocks.shape
    W = nb * tn
    return pl.pallas_call(
        _inproj_kernel,
        grid=(T // tm, W // tn),
        in_specs=[
            pl.BlockSpec((tm, D), lambda i, j: (i, 0)),
            pl.BlockSpec((1, D), lambda i, j: (0, 0)),
            pl.BlockSpec((None, D, tn), lambda i, j: (j, 0, 0)),
        ],
        out_specs=pl.BlockSpec((tm, tn), lambda i, j: (i, j)),
        out_shape=jax.ShapeDtypeStruct((T, W), BF16),
        scratch_shapes=[pltpu.VMEM((tm, D), BF16)],
        compiler_params=_params(("parallel", "arbitrary")),
        name="in_proj",
    )(x2, gain, w_blocks)


def _attn_kernel(sinks_ref, q_ref, kc_ref, kp_ref, vc_ref, vp_ref,
                 cos_ref, sin_ref, cosp_ref, sinp_ref, o_ref):
    n = pl.program_id(1)
    C = CHUNK
    cos, sin = cos_ref[...], sin_ref[...]
    cosp, sinp = cosp_ref[...], sinp_ref[...]
    qi = lax.broadcasted_iota(jnp.int32, (C, 2 * C), 0)
    kj = lax.broadcasted_iota(jnp.int32, (C, 2 * C), 1)
    diff = qi + C - kj
    in_band = jnp.where(diff >= 0, jnp.where(diff < WINDOW, 1, 0), 0)
    has_prev = jnp.where(n > 0, 1, 0)
    allowed = in_band * jnp.where(kj >= C, 1, has_prev) > 0

    for hk in range(ATTN_KV_HEADS):
        ksl = slice(hk * HEAD_DIM, (hk + 1) * HEAD_DIM)
        kc = _rope(kc_ref[:, ksl].astype(F32), cos, sin)
        kp = _rope(kp_ref[:, ksl].astype(F32), cosp, sinp)
        kb = jnp.concatenate([kp, kc], axis=0).astype(BF16)
        vb = jnp.concatenate([vp_ref[:, ksl], vc_ref[:, ksl]], axis=0)
        for g in range(ATTN_GROUP):
            h = hk * ATTN_GROUP + g
            hsl = slice(h * HEAD_DIM, (h + 1) * HEAD_DIM)
            q = _rope(q_ref[:, hsl].astype(F32), cos, sin).astype(BF16)
            s = lax.dot_general(q, kb, (((1,), (1,)), ((), ())), preferred_element_type=F32)
            s = jnp.where(allowed, s * (HEAD_DIM ** -0.5), -1e30)
            sink = sinks_ref[h]
            m = jnp.maximum(jnp.max(s, axis=-1, keepdims=True), sink)
            p = jnp.exp(s - m)
            denom = jnp.sum(p, axis=-1, keepdims=True) + jnp.exp(sink - m)
            o = jnp.dot(p.astype(BF16), vb, preferred_element_type=F32)
            o_ref[:, hsl] = (o / denom).astype(o_ref.dtype)


def _attention(proj, sinks, cos_full, sin_signed, B, S):
    C = CHUNK
    N = S // C
    qa_blk = COL_QA // (ATTN_Q_HEADS * HEAD_DIM)
    ka_blk = COL_KA // (ATTN_KV_HEADS * HEAD_DIM)
    va_blk = COL_VA // (ATTN_KV_HEADS * HEAD_DIM)
    kvw = ATTN_KV_HEADS * HEAD_DIM
    cur = lambda b, n: b * N + n
    prev = lambda b, n: b * N + jnp.maximum(n - 1, 0)
    return pl.pallas_call(
        _attn_kernel,
        grid=(B, N),
        in_specs=[
            pl.BlockSpec(memory_space=pltpu.SMEM),
            pl.BlockSpec((C, ATTN_Q_HEADS * HEAD_DIM), lambda b, n: (cur(b, n), qa_blk)),
            pl.BlockSpec((C, kvw), lambda b, n: (cur(b, n), ka_blk)),
            pl.BlockSpec((C, kvw), lambda b, n: (prev(b, n), ka_blk)),
            pl.BlockSpec((C, kvw), lambda b, n: (cur(b, n), va_blk)),
            pl.BlockSpec((C, kvw), lambda b, n: (prev(b, n), va_blk)),
            pl.BlockSpec((C, HEAD_DIM), lambda b, n: (n, 0)),
            pl.BlockSpec((C, HEAD_DIM), lambda b, n: (n, 0)),
            pl.BlockSpec((C, HEAD_DIM), lambda b, n: (jnp.maximum(n - 1, 0), 0)),
            pl.BlockSpec((C, HEAD_DIM), lambda b, n: (jnp.maximum(n - 1, 0), 0)),
        ],
        out_specs=pl.BlockSpec((C, ATTN_Q_HEADS * HEAD_DIM), lambda b, n: (cur(b, n), 0)),
        out_shape=jax.ShapeDtypeStruct((B * S, ATTN_Q_HEADS * HEAD_DIM), BF16),
        compiler_params=_params(("parallel", "arbitrary")),
        name="swa_attention",
    )(sinks, proj, proj, proj, proj, proj, cos_full, sin_signed, cos_full, sin_signed)


def _ret_kernel(cdec_ref, q_ref, k_ref, v_ref, g_ref, cos_ref, sin_ref,
                dmask_ref, qdec_ref, kdec_ref, o_ref, state_ref):
    @pl.when(pl.program_id(1) == 0)
    def _():
        state_ref[...] = jnp.zeros_like(state_ref)

    cos, sin = cos_ref[...], sin_ref[...]
    for h in range(RET_HEADS):
        qsl = slice(h * RET_QK_DIM, (h + 1) * RET_QK_DIM)
        vsl = slice(h * RET_V_DIM, (h + 1) * RET_V_DIM)
        q = _rope(q_ref[:, qsl].astype(F32), cos, sin)
        k = _rope(k_ref[:, qsl].astype(F32), cos, sin) * (RET_QK_DIM ** -0.5)
        v = v_ref[:, vsl]
        inner = lax.dot_general(q.astype(BF16), k.astype(BF16), (((1,), (1,)), ((), ())),
                                preferred_element_type=F32) * dmask_ref[h]
        o = jnp.dot(inner.astype(BF16), v, preferred_element_type=F32)
        state = state_ref[h]
        o = o + jnp.dot((q * qdec_ref[h]).astype(BF16), state.astype(BF16),
                        preferred_element_type=F32)
        kd_t = (k * kdec_ref[h]).T.astype(BF16)
        state_ref[h] = state * cdec_ref[h] + jnp.dot(kd_t, v, preferred_element_type=F32)
        mu = jnp.mean(o, axis=-1, keepdims=True)
        oc = o - mu
        var = jnp.mean(oc * oc, axis=-1, keepdims=True)
        gate = g_ref[:, vsl].astype(F32)
        o_ref[:, vsl] = (oc * lax.rsqrt(var + NORM_EPS) * (gate * jax.nn.sigmoid(gate))).astype(o_ref.dtype)


def _retention(proj, cos_full, sin_signed, B, S):
    C = CHUNK
    N = S // C
    H = RET_HEADS
    log_gamma = jnp.log(1.0 - 2.0 ** (-5.0 - jnp.arange(H, dtype=F32)))
    pos = jnp.arange(C, dtype=F32)
    d = pos[:, None] - pos[None, :]
    dmask = jnp.where(d >= 0, jnp.exp(log_gamma[:, None, None] * jnp.maximum(d, 0.0)), 0.0)
    kdec = jnp.exp(log_gamma[:, None] * (C - 1.0 - pos)[None, :])
    qdec = jnp.exp(log_gamma[:, None] * (pos + 1.0)[None, :])
    cdec = jnp.exp(log_gamma * C)
    qdec_b = jnp.broadcast_to(qdec[:, :, None], (H, C, RET_QK_DIM))
    kdec_b = jnp.broadcast_to(kdec[:, :, None], (H, C, RET_QK_DIM))
    qkw, vw = H * RET_QK_DIM, H * RET_V_DIM
    row = lambda b, n: b * N + n
    full3 = lambda b, n: (0, 0, 0)
    return pl.pallas_call(
        _ret_kernel,
        grid=(B, N),
        in_specs=[
            pl.BlockSpec(memory_space=pltpu.SMEM),
            pl.BlockSpec((C, qkw), lambda b, n: (row(b, n), COL_QR // qkw)),
            pl.BlockSpec((C, qkw), lambda b, n: (row(b, n), COL_KR // qkw)),
            pl.BlockSpec((C, vw), lambda b, n: (row(b, n), COL_VR // vw)),
            pl.BlockSpec((C, vw), lambda b, n: (row(b, n), COL_GR // vw)),
            pl.BlockSpec((C, HEAD_DIM), lambda b, n: (n, 0)),
            pl.BlockSpec((C, HEAD_DIM), lambda b, n: (n, 0)),
            pl.BlockSpec((H, C, C), full3),
            pl.BlockSpec((H, C, RET_QK_DIM), full3),
            pl.BlockSpec((H, C, RET_QK_DIM), full3),
        ],
        out_specs=pl.BlockSpec((C, vw), lambda b, n: (row(b, n), 0)),
        out_shape=jax.ShapeDtypeStruct((B * S, vw), BF16),
        scratch_shapes=[pltpu.VMEM((H, RET_QK_DIM, RET_V_DIM), F32)],
        compiler_params=_params(("parallel", "arbitrary")),
        name="retention",
    )(cdec, proj, proj, proj, proj, cos_full, sin_signed, dmask, qdec_b, kdec_b)


def _merge_kernel(ya_ref, yr_ref, wa_ref, wr_ref, ga_ref, gr_ref, o_ref):
    pa = jnp.dot(ya_ref[...], wa_ref[...], preferred_element_type=F32)
    pr = jnp.dot(yr_ref[...], wr_ref[...], preferred_element_type=F32)
    ga = jax.nn.sigmoid(ga_ref[...].astype(F32))
    gr = jax.nn.sigmoid(gr_ref[...].astype(F32))
    o_ref[...] = (ga * pa + gr * pr).astype(o_ref.dtype)


def _merge(ya, yr, wa_bf, wr_bf, proj, tm, tn):
    T = ya.shape[0]
    D = wa_bf.shape[1]
    return pl.pallas_call(
        _merge_kernel,
        grid=(T // tm, D // tn),
        in_specs=[
            pl.BlockSpec((tm, ya.shape[1]), lambda i, j: (i, 0)),
            pl.BlockSpec((tm, yr.shape[1]), lambda i, j: (i, 0)),
            pl.BlockSpec((wa_bf.shape[0], tn), lambda i, j: (0, j)),
            pl.BlockSpec((wr_bf.shape[0], tn), lambda i, j: (0, j)),
            pl.BlockSpec((tm, tn), lambda i, j: (i, COL_GA // tn + j)),
            pl.BlockSpec((tm, tn), lambda i, j: (i, COL_GRT // tn + j)),
        ],
        out_specs=pl.BlockSpec((tm, tn), lambda i, j: (i, j)),
        out_shape=jax.ShapeDtypeStruct((T, D), BF16),
        compiler_params=_params(("parallel", "arbitrary")),
        name="merge_branches",
    )(ya, yr, wa_bf, wr_bf, proj, proj)


def _outproj_kernel(x_ref, m_ref, w_ref, g_ref, h_ref, hnt_ref):
    h = x_ref[...] + jnp.dot(m_ref[...], w_ref[...], preferred_element_type=F32)
    h_ref[...] = h
    ms = jnp.mean(h * h, axis=-1, keepdims=True)
    hn = h * lax.rsqrt(ms + NORM_EPS) * g_ref[...]
    hnt_ref[...] = hn.T.astype(hnt_ref.dtype)


def _outproj(x2, merged, w_bf, gain, tm):
    T, D = x2.shape
    return pl.pallas_call(
        _outproj_kernel,
        grid=(T // tm,),
        in_specs=[
            pl.BlockSpec((tm, D), lambda i: (i, 0)),
            pl.BlockSpec((tm, D), lambda i: (i, 0)),
            pl.BlockSpec((D, D), lambda i: (0, 0)),
            pl.BlockSpec((1, D), lambda i: (0, 0)),
        ],
        out_specs=[
            pl.BlockSpec((tm, D), lambda i: (i, 0)),
            pl.BlockSpec((D, tm), lambda i: (0, i)),
        ],
        out_shape=[
            jax.ShapeDtypeStruct((T, D), F32),
            jax.ShapeDtypeStruct((D, T), BF16),
        ],
        compiler_params=_params(("parallel",)),
        name="out_proj",
    )(x2, merged, w_bf, gain)


def _staircase():
    K = PEER_TOPK
    return [(k1, k2) for k1 in range(K) for k2 in range(K) if (k1 + 1) * (k2 + 1) <= K + 1]


def _batcher_pairs(n):
    pairs, p = [], 1
    while p < n:
        k = p
        while k >= 1:
            for j in range(k % p, n - k, 2 * k):
                for i in range(min(k, n - j - k)):
                    if (i + j) // (2 * p) == (i + j + k) // (2 * p):
                        pairs.append((i + j, i + j + k))
            k //= 2
        p *= 2
    return pairs


def _compare_exchange(v, i, j):
    v[i], v[j] = jnp.maximum(v[i], v[j]), jnp.minimum(v[i], v[j])


def _sort_desc_padded(vals):
    n = 1
    while n < len(vals):
        n *= 2
    v = list(vals) + [None] * (n - len(vals))
    for i, j in _batcher_pairs(n):
        if v[j] is None:
            continue
        if v[i] is None:
            v[i], v[j] = v[j], None
            continue
        _compare_exchange(v, i, j)
    return v


def _sorted_top16(s_ref, ls):
    H, NK, K = PEER_HEADS, PEER_N_KEYS, PEER_TOPK
    sort_net = _batcher_pairs(K)
    top = None
    for g in range(NK // K):
        v = [s_ref[(g * K + i) * H:(g * K + i + 1) * H, ls] for i in range(K)]
        for i, j in sort_net:
            _compare_exchange(v, i, j)
        if top is None:
            top = v
            continue
        top = [jnp.maximum(top[i], v[K - 1 - i]) for i in range(K)]
        k = K // 2
        while k >= 1:
            for i in range(K):
                if i & k == 0:
                    _compare_exchange(top, i, i + k)
            k //= 2
    return top


def _routing_kernel(hnt_ref, wq_ref, kbig_ref, a_ref, b_ref, th_ref, s1o, s2o, bscr):
    H, NK, K = PEER_HEADS, PEER_N_KEYS, PEER_TOPK
    tb = hnt_ref.shape[1]
    half = H * PEER_HALF_DIM
    qt = jnp.dot(wq_ref[...], hnt_ref[...], preferred_element_type=F32).astype(BF16)
    s1o[...] = jnp.dot(kbig_ref[0], qt[:half], preferred_element_type=F32)
    s2o[...] = jnp.dot(kbig_ref[1], qt[half:], preferred_element_type=F32)

    pairs = _staircase()
    for lt in range(tb // LANES):
        ls = slice(lt * LANES, (lt + 1) * LANES)
        t1v = _sorted_top16(s1o, ls)
        t2v = _sorted_top16(s2o, ls)
        cand = [t1v[k1] + t2v[k2] for (k1, k2) in pairs]
        cmax = cand[0]
        tops = _sort_desc_padded(cand)[:K + 1]
        zsum = jnp.zeros((H, LANES), F32)
        for r in range(K):
            zsum = zsum + jnp.exp(tops[r] - cmax)
        inv_z = 1.0 / zsum
        th_ref[:, ls] = jnp.exp(0.5 * (tops[K - 1] + tops[K]) - cmax) * inv_z
        m1, m2 = t1v[0], t2v[0]
        for n in range(NK):
            rows = slice(n * H, (n + 1) * H)
            s1, s2 = s1o[rows, ls], s2o[rows, ls]
            a_ref[rows, ls] = jnp.where(s1 >= t1v[K - 1], jnp.exp(s1 - m1) * inv_z, 0.0)
            bscr[lt, rows, :] = jnp.where(s2 >= t2v[K - 1], jnp.exp(s2 - m2), 0.0)
        for h in range(H):
            b_ref[h * NK:(h + 1) * NK, ls] = bscr[lt, pl.ds(h, NK, stride=H), :]


def _routing(hnt, wq_t, kbig, tb):
    D, T = hnt.shape
    R = PEER_HEADS * PEER_N_KEYS
    scr = lambda rows: pltpu.VMEM((rows, tb), F32)
    return pl.pallas_call(
        _routing_kernel,
        grid=(T // tb,),
        in_specs=[
            pl.BlockSpec((D, tb), lambda i: (0, i)),
            pl.BlockSpec(wq_t.shape, lambda i: (0, 0)),
            pl.BlockSpec(kbig.shape, lambda i: (0, 0, 0)),
        ],
        out_specs=[
            pl.BlockSpec((R, tb), lambda i: (0, i)),
            pl.BlockSpec((R, tb), lambda i: (0, i)),
            pl.BlockSpec((PEER_HEADS, tb), lambda i: (0, i)),
        ],
        out_shape=[
            jax.ShapeDtypeStruct((R, T), F32),
            jax.ShapeDtypeStruct((R, T), F32),
            jax.ShapeDtypeStruct((PEER_HEADS, T), F32),
        ],
        scratch_shapes=[scr(R), scr(R), pltpu.VMEM((tb // LANES, R, LANES), F32)],
        compiler_params=_params(("parallel",)),
        name="peer_routing",
    )(hnt, wq_t, kbig)


def _gelu(x):
    return 0.5 * x * (1.0 + lax.erf(x * (2.0 ** -0.5)))


def _experts_kernel(hnt_ref, down_ref, upt_ref, a_ref, b_ref, th_ref, yt_ref,
                    arep, threp, p_scr, *, sub, halves):
    H, NK, SL = PEER_HEADS, PEER_N_KEYS, SUBLANES
    ec, tb = down_ref.shape[0], hnt_ref.shape[1]

    @pl.when(pl.program_id(1) == 0)
    def _():
        yt_ref[...] = jnp.zeros_like(yt_ref)
        for h in range(H):
            threp[h] = jnp.broadcast_to(th_ref[h:h + 1, :], (SL, tb))

    for r in range(ec // NK * H):
        arep[r] = jnp.broadcast_to(a_ref[r:r + 1, :], (SL, tb))

    hnt = hnt_ref[...]
    n_sub = ec // sub
    for s in range(n_sub):
        at = jnp.dot(down_ref[s * sub:(s + 1) * sub, :], hnt, preferred_element_type=F32)
        for c in range(sub // NK):
            i1 = s * (sub // NK) + c
            for lt in range(tb // LANES):
                ls = slice(lt * LANES, (lt + 1) * LANES)
                for g2 in range(NK // (2 * SL)):
                    tiles = []
                    for g in (2 * g2, 2 * g2 + 1):
                        w = jnp.zeros((SL, LANES), F32)
                        for h in range(H):
                            prod = arep[i1 * H + h, :, ls] * b_ref[h * NK + g * SL:h * NK + (g + 1) * SL, ls]
                            w = w + jnp.where(prod >= threp[h, :, ls], prod, 0.0)
                        r0 = c * NK + g * SL
                        tiles.append(w * _gelu(at[r0:r0 + SL, ls]))
                    r0 = s * sub + c * NK + 2 * g2 * SL
                    p_scr[r0:r0 + 2 * SL, ls] = jnp.concatenate(tiles, axis=0).astype(BF16)
        if (s + 1) % (n_sub // halves) == 0:
            k0, k1 = (s + 1 - n_sub // halves) * sub, (s + 1) * sub
            yt_ref[...] += jnp.dot(upt_ref[:, k0:k1], p_scr[k0:k1, :], preferred_element_type=F32)


def _experts(hnt, down_bf, up, a_t, b_t, th_t, tb, ec, sub=512, halves=1):
    D, T = hnt.shape
    NE = down_bf.shape[0]
    a_rows = ec // PEER_N_KEYS * PEER_HEADS
    upt_bf = up.reshape(NE // ec, ec, D).transpose(0, 2, 1).astype(BF16)
    return pl.pallas_call(
        functools.partial(_experts_kernel, sub=sub, halves=halves),
        grid=(T // tb, NE // ec),
        in_specs=[
            pl.BlockSpec((D, tb), lambda t, e: (0, t)),
            pl.BlockSpec((ec, D), lambda t, e: (e, 0)),
            pl.BlockSpec((None, D, ec), lambda t, e: (e, 0, 0)),
            pl.BlockSpec((a_rows, tb), lambda t, e: (e, t)),
            pl.BlockSpec((b_t.shape[0], tb), lambda t, e: (0, t)),
            pl.BlockSpec((PEER_HEADS, tb), lambda t, e: (0, t)),
        ],
        out_specs=pl.BlockSpec((D, tb), lambda t, e: (0, t)),
        out_shape=jax.ShapeDtypeStruct((D, T), F32),
        scratch_shapes=[pltpu.VMEM((a_rows, SUBLANES, tb), F32),
                        pltpu.VMEM((PEER_HEADS, SUBLANES, tb), F32),
                        pltpu.VMEM((ec, tb), BF16)],
        compiler_params=_params(("parallel", "arbitrary")),
        name="peer_experts",
    )(hnt, down_bf, upt_bf, a_t, b_t, th_t)


def _finish_kernel(h_ref, yt_ref, g_ref, o_ref):
    h = h_ref[...] + yt_ref[...].T
    ms = jnp.mean(h * h, axis=-1, keepdims=True)
    o_ref[...] = h * lax.rsqrt(ms + NORM_EPS) * g_ref[...]


def _finish(h, yt, gain, tm):
    T, D = h.shape
    return pl.pallas_call(
        _finish_kernel,
        grid=(T // tm,),
        in_specs=[
            pl.BlockSpec((tm, D), lambda i: (i, 0)),
            pl.BlockSpec((D, tm), lambda i: (0, i)),
            pl.BlockSpec((1, D), lambda i: (0, 0)),
        ],
        out_specs=pl.BlockSpec((tm, D), lambda i: (i, 0)),
        out_shape=jax.ShapeDtypeStruct((T, D), F32),
        compiler_params=_params(("parallel",)),
        name="finish",
    )(h, yt, gain)


def _w_in_block_table(tn):
    src_off = {"qa": 0, "kv": 1024, "qr": 1536, "kr": 2560, "vr": 3584, "gr": 5632, "ga": 7680, "grt": 9728}
    width = {"qa": 1024, "kv": 512, "qr": 1024, "kr": 1024, "vr": 2048, "gr": 2048, "ga": 2048, "grt": 2048}
    table = []
    for name in ("vr", "gr", "ga", "grt", "qa", "qr", "kr", "kv"):
        assert src_off[name] % tn == 0 and width[name] % tn == 0
        table += [src_off[name] // tn + i for i in range(width[name] // tn)]
    return table


def _pack_kernel(w_ref, o_ref):
    o_ref[...] = w_ref[...].astype(o_ref.dtype)


def _pack_w_in(w, tn):
    D, W = w.shape
    table = _w_in_block_table(tn)

    def src_block(j):
        blk = jnp.int32(0)
        for k, t in enumerate(table):
            blk = jnp.where(j == k, t, blk)
        return blk

    return pl.pallas_call(
        _pack_kernel,
        grid=(W // tn,),
        in_specs=[pl.BlockSpec((D, tn), lambda j: (0, src_block(j)))],
        out_specs=pl.BlockSpec((None, D, tn), lambda j: (j, 0, 0)),
        out_shape=jax.ShapeDtypeStruct((W // tn, D, tn), BF16),
        compiler_params=_params(("parallel",)),
        name="pack_w_in",
    )(w)


def _rope_tables(S):
    pos = jnp.arange(S, dtype=F32)
    inv_freq = 1.0 / (ROPE_THETA ** (jnp.arange(0, HEAD_DIM, 2, dtype=F32) / HEAD_DIM))
    ang = pos[:, None] * inv_freq[None, :]
    cos, sin = jnp.cos(ang), jnp.sin(ang)
    return jnp.concatenate([cos, cos], axis=1), jnp.concatenate([-sin, sin], axis=1)


def kernel(x, attn_norm, w_in, attn_sinks, w_attn_branch, w_ret_branch, w_out, ffn_norm,
           w_peer_query, peer_sub_keys, peer_expert_down, peer_expert_up, final_norm):
    B, S, D = x.shape
    T = B * S
    assert w_in.shape[0] == 1, "single-layer block"
    assert S % CHUNK == 0 and T % 512 == 0

    x2 = x.reshape(T, D)
    cos_full, sin_signed = _rope_tables(S)

    proj = _inproj(x2, attn_norm[0][None, :], _pack_w_in(w_in[0], tn=512), tm=1024)
    y_a = _attention(proj, attn_sinks[0], cos_full, sin_signed, B, S)
    y_r = _retention(proj, cos_full, sin_signed, B, S)
    merged = _merge(y_a, y_r, w_attn_branch[0].astype(BF16), w_ret_branch[0].astype(BF16),
                    proj, tm=512, tn=512)
    h, hnt = _outproj(x2, merged, w_out[0].astype(BF16), ffn_norm[0][None, :], tm=256)

    H, NK, HD = PEER_HEADS, PEER_N_KEYS, PEER_HALF_DIM
    wq_t = w_peer_query[0].reshape(D, H, 2, HD).transpose(2, 1, 3, 0).reshape(2 * H * HD, D).astype(BF16)
    eye = jnp.eye(H, dtype=F32)
    kbig = jnp.einsum("phnd,hg->pnhgd", peer_sub_keys[0], eye).reshape(2, NK * H, H * HD).astype(BF16)
    a_t, b_t, th_t = _routing(hnt, wq_t, kbig, tb=256)

    yt = _experts(hnt, peer_expert_down[0].astype(BF16), peer_expert_up[0],
                  a_t, b_t, th_t, tb=512, ec=1024)
    out = _finish(h, yt, final_norm[None, :], tm=256)
    return out.reshape(B, S, D)
```

```python
import functools

import jax
import jax.numpy as jnp
from jax import lax
from jax.experimental import pallas as pl
from jax.experimental.pallas import tpu as pltpu

F32 = jnp.float32
BF16 = jnp.bfloat16

HEAD_DIM = 128
ATTN_Q_HEADS = 8
ATTN_KV_HEADS = 2
ATTN_GROUP = ATTN_Q_HEADS // ATTN_KV_HEADS
WINDOW = 128
CHUNK = 128
ROPE_THETA = 10000.0
RET_HEADS = 8
RET_QK_DIM = 128
RET_V_DIM = 256
PEER_HEADS = 8
PEER_N_KEYS = 128
PEER_TOPK = 16
PEER_HALF_DIM = 128
NORM_EPS = 1e-6
NEG_INF = float("-inf")

LANES = 128
SUBLANES = 8
VMEM_LIMIT = 48 * 1024 * 1024
EXPERTS_VMEM_LIMIT = 58 * 1024 * 1024

SEG_W2048 = 2048
COL_VR, COL_GR, COL_GA, COL_GRT = 0, 2048, 4096, 6144
COL_QA, COL_QR, COL_KR = 8192, 9216, 10240
COL_KA, COL_VA = 11264, 11520
IN_WIDTH = 11776


def _params(sem, vmem_limit=VMEM_LIMIT):
    return pltpu.CompilerParams(dimension_semantics=sem, vmem_limit_bytes=vmem_limit)


def _rope(x, cos_full, sin_signed):
    return x * cos_full + pltpu.roll(x, HEAD_DIM // 2, axis=1) * sin_signed


def _inproj_kernel(x_ref, g_ref, w_ref, o_ref, xn_ref):
    @pl.when(pl.program_id(1) == 0)
    def _():
        x = x_ref[...]
        ms = jnp.mean(x * x, axis=-1, keepdims=True)
        xn_ref[...] = (x * lax.rsqrt(ms + NORM_EPS) * g_ref[...]).astype(BF16)

    o_ref[...] = jnp.dot(xn_ref[...], w_ref[...], preferred_element_type=F32).astype(o_ref.dtype)


def _inproj(x2, gain, w_blocks, tm):
    T, D = x2.shape
    nb, _, tn = w_blocks.shape
    W = nb * tn
    return pl.pallas_call(
        _inproj_kernel,
        grid=(T // tm, W // tn),
        in_specs=[
            pl.BlockSpec((tm, D), lambda i, j: (i, 0)),
            pl.BlockSpec((1, D), lambda i, j: (0, 0)),
            pl.BlockSpec((None, D, tn), lambda i, j: (j, 0, 0)),
        ],
        out_specs=pl.BlockSpec((tm, tn), lambda i, j: (i, j)),
        out_shape=jax.ShapeDtypeStruct((T, W), BF16),
        scratch_shapes=[pltpu.VMEM((tm, D), BF16)],
        compiler_params=_params(("parallel", "arbitrary")),
        name="in_proj",
    )(x2, gain, w_blocks)


def _attn_kernel(sinks_ref, q_ref, kc_ref, kp_ref, vc_ref, vp_ref,
                 cos_ref, sin_ref, cosp_ref, sinp_ref, o_ref):
    n = pl.program_id(1)
    C = CHUNK
    cos, sin = cos_ref[...], sin_ref[...]
    cosp, sinp = cosp_ref[...], sinp_ref[...]
    qi = lax.broadcasted_iota(jnp.int32, (C, 2 * C), 0)
    kj = lax.broadcasted_iota(jnp.int32, (C, 2 * C), 1)
    diff = qi + C - kj
    in_band = jnp.where(diff >= 0, jnp.where(diff < WINDOW, 1, 0), 0)
    has_prev = jnp.where(n > 0, 1, 0)
    allowed = in_band * jnp.where(kj >= C, 1, has_prev) > 0

    for hk in range(ATTN_KV_HEADS):
        ksl = slice(hk * HEAD_DIM, (hk + 1) * HEAD_DIM)
        kc = _rope(kc_ref[:, ksl].astype(F32), cos, sin)
        kp = _rope(kp_ref[:, ksl].astype(F32), cosp, sinp)
        kb = jnp.concatenate([kp, kc], axis=0).astype(BF16)
        vb = jnp.concatenate([vp_ref[:, ksl], vc_ref[:, ksl]], axis=0)
        for g in range(ATTN_GROUP):
            h = hk * ATTN_GROUP + g
            hsl = slice(h * HEAD_DIM, (h + 1) * HEAD_DIM)
            q = _rope(q_ref[:, hsl].astype(F32), cos, sin).astype(BF16)
            s = lax.dot_general(q, kb, (((1,), (1,)), ((), ())), preferred_element_type=F32)
            s = jnp.where(allowed, s * (HEAD_DIM ** -0.5), -1e30)
            sink = sinks_ref[h]
            m = jnp.maximum(jnp.max(s, axis=-1, keepdims=True), sink)
            p = jnp.exp(s - m)
            denom = jnp.sum(p, axis=-1, keepdims=True) + jnp.exp(sink - m)
            o = jnp.dot(p.astype(BF16), vb, preferred_element_type=F32)
            o_ref[:, hsl] = (o / denom).astype(o_ref.dtype)


def _attention(proj, sinks, cos_full, sin_signed, B, S):
    C = CHUNK
    N = S // C
    qa_blk = COL_QA // (ATTN_Q_HEADS * HEAD_DIM)
    ka_blk = COL_KA // (ATTN_KV_HEADS * HEAD_DIM)
    va_blk = COL_VA // (ATTN_KV_HEADS * HEAD_DIM)
    kvw = ATTN_KV_HEADS * HEAD_DIM
    cur = lambda b, n: b * N + n
    prev = lambda b, n: b * N + jnp.maximum(n - 1, 0)
    return pl.pallas_call(
        _attn_kernel,
        grid=(B, N),
        in_specs=[
            pl.BlockSpec(memory_space=pltpu.SMEM),
            pl.BlockSpec((C, ATTN_Q_HEADS * HEAD_DIM), lambda b, n: (cur(b, n), qa_blk)),
            pl.BlockSpec((C, kvw), lambda b, n: (cur(b, n), ka_blk)),
            pl.BlockSpec((C, kvw), lambda b, n: (prev(b, n), ka_blk)),
            pl.BlockSpec((C, kvw), lambda b, n: (cur(b, n), va_blk)),
            pl.BlockSpec((C, kvw), lambda b, n: (prev(b, n), va_blk)),
            pl.BlockSpec((C, HEAD_DIM), lambda b, n: (n, 0)),
            pl.BlockSpec((C, HEAD_DIM), lambda b, n: (n, 0)),
            pl.BlockSpec((C, HEAD_DIM), lambda b, n: (jnp.maximum(n - 1, 0), 0)),
            pl.BlockSpec((C, HEAD_DIM), lambda b, n: (jnp.maximum(n - 1, 0), 0)),
        ],
        out_specs=pl.BlockSpec((C, ATTN_Q_HEADS * HEAD_DIM), lambda b, n: (cur(b, n), 0)),
        out_shape=jax.ShapeDtypeStruct((B * S, ATTN_Q_HEADS * HEAD_DIM), BF16),
        compiler_params=_params(("parallel", "arbitrary")),
        name="swa_attention",
    )(sinks, proj, proj, proj, proj, proj, cos_full, sin_signed, cos_full, sin_signed)


def _ret_kernel(cdec_ref, q_ref, k_ref, v_ref, g_ref, cos_ref, sin_ref,
                dmask_ref, qdec_ref, kdec_ref, o_ref, state_ref):
    @pl.when(pl.program_id(1) == 0)
    def _():
        state_ref[...] = jnp.zeros_like(state_ref)

    cos, sin = cos_ref[...], sin_ref[...]
    for h in range(RET_HEADS):
        qsl = slice(h * RET_QK_DIM, (h + 1) * RET_QK_DIM)
        vsl = slice(h * RET_V_DIM, (h + 1) * RET_V_DIM)
        q = _rope(q_ref[:, qsl].astype(F32), cos, sin)
        k = _rope(k_ref[:, qsl].astype(F32), cos, sin) * (RET_QK_DIM ** -0.5)
        v = v_ref[:, vsl]
        inner = lax.dot_general(q.astype(BF16), k.astype(BF16), (((1,), (1,)), ((), ())),
                                preferred_element_type=F32) * dmask_ref[h]
        o = jnp.dot(inner.astype(BF16), v, preferred_element_type=F32)
        state = state_ref[h]
        o = o + jnp.dot((q * qdec_ref[h]).astype(BF16), state.astype(BF16),
                        preferred_element_type=F32)
        kd_t = (k * kdec_ref[h]).T.astype(BF16)
        state_ref[h] = state * cdec_ref[h] + jnp.dot(kd_t, v, preferred_element_type=F32)
        mu = jnp.mean(o, axis=-1, keepdims=True)
        oc = o - mu
        var = jnp.mean(oc * oc, axis=-1, keepdims=True)
        gate = g_ref[:, vsl].astype(F32)
        o_ref[:, vsl] = (oc * lax.rsqrt(var + NORM_EPS) * (gate * jax.nn.sigmoid(gate))).astype(o_ref.dtype)


def _retention(proj, cos_full, sin_signed, B, S):
    C = CHUNK
    N = S // C
    H = RET_HEADS
    log_gamma = jnp.log(1.0 - 2.0 ** (-5.0 - jnp.arange(H, dtype=F32)))
    pos = jnp.arange(C, dtype=F32)
    d = pos[:, None] - pos[None, :]
    dmask = jnp.where(d >= 0, jnp.exp(log_gamma[:, None, None] * jnp.maximum(d, 0.0)), 0.0)
    kdec = jnp.exp(log_gamma[:, None] * (C - 1.0 - pos)[None, :])
    qdec = jnp.exp(log_gamma[:, None] * (pos + 1.0)[None, :])
    cdec = jnp.exp(log_gamma * C)
    qdec_b = jnp.broadcast_to(qdec[:, :, None], (H, C, RET_QK_DIM))
    kdec_b = jnp.broadcast_to(kdec[:, :, None], (H, C, RET_QK_DIM))
    qkw, vw = H * RET_QK_DIM, H * RET_V_DIM
    row = lambda b, n: b * N + n
    full3 = lambda b, n: (0, 0, 0)
    return pl.pallas_call(
        _ret_kernel,
        grid=(B, N),
        in_specs=[
            pl.BlockSpec(memory_space=pltpu.SMEM),
            pl.BlockSpec((C, qkw), lambda b, n: (row(b, n), COL_QR // qkw)),
            pl.BlockSpec((C, qkw), lambda b, n: (row(b, n), COL_KR // qkw)),
            pl.BlockSpec((C, vw), lambda b, n: (row(b, n), COL_VR // vw)),
            pl.BlockSpec((C, vw), lambda b, n: (row(b, n), COL_GR // vw)),
            pl.BlockSpec((C, HEAD_DIM), lambda b, n: (n, 0)),
            pl.BlockSpec((C, HEAD_DIM), lambda b, n: (n, 0)),
            pl.BlockSpec((H, C, C), full3),
            pl.BlockSpec((H, C, RET_QK_DIM), full3),
            pl.BlockSpec((H, C, RET_QK_DIM), full3),
        ],
        out_specs=pl.BlockSpec((C, vw), lambda b, n: (row(b, n), 0)),
        out_shape=jax.ShapeDtypeStruct((B * S, vw), BF16),
        scratch_shapes=[pltpu.VMEM((H, RET_QK_DIM, RET_V_DIM), F32)],
        compiler_params=_params(("parallel", "arbitrary")),
        name="retention",
    )(cdec, proj, proj, proj, proj, cos_full, sin_signed, dmask, qdec_b, kdec_b)


def _merge_kernel(ya_ref, yr_ref, wa_ref, wr_ref, ga_ref, gr_ref, o_ref):
    pa = jnp.dot(ya_ref[...], wa_ref[...], preferred_element_type=F32)
    pr = jnp.dot(yr_ref[...], wr_ref[...], preferred_element_type=F32)
    ga = jax.nn.sigmoid(ga_ref[...].astype(F32))
    gr = jax.nn.sigmoid(gr_ref[...].astype(F32))
    o_ref[...] = (ga * pa + gr * pr).astype(o_ref.dtype)


def _merge(ya, yr, wa_bf, wr_bf, proj, tm, tn):
    T = ya.shape[0]
    D = wa_bf.shape[1]
    return pl.pallas_call(
        _merge_kernel,
        grid=(T // tm, D // tn),
        in_specs=[
            pl.BlockSpec((tm, ya.shape[1]), lambda i, j: (i, 0)),
            pl.BlockSpec((tm, yr.shape[1]), lambda i, j: (i, 0)),
            pl.BlockSpec((wa_bf.shape[0], tn), lambda i, j: (0, j)),
            pl.BlockSpec((wr_bf.shape[0], tn), lambda i, j: (0, j)),
            pl.BlockSpec((tm, tn), lambda i, j: (i, COL_GA // tn + j)),
            pl.BlockSpec((tm, tn), lambda i, j: (i, COL_GRT // tn + j)),
        ],
        out_specs=pl.BlockSpec((tm, tn), lambda i, j: (i, j)),
        out_shape=jax.ShapeDtypeStruct((T, D), BF16),
        compiler_params=_params(("parallel", "arbitrary")),
        name="merge_branches",
    )(ya, yr, wa_bf, wr_bf, proj, proj)


def _outproj_kernel(x_ref, m_ref, w_ref, g_ref, h_ref, hnt_ref):
    h = x_ref[...] + jnp.dot(m_ref[...], w_ref[...], preferred_element_type=F32)
    h_ref[...] = h
    ms = jnp.mean(h * h, axis=-1, keepdims=True)
    hn = h * lax.rsqrt(ms + NORM_EPS) * g_ref[...]
    hnt_ref[...] = hn.T.astype(hnt_ref.dtype)


def _outproj(x2, merged, w_bf, gain, tm):
    T, D = x2.shape
    return pl.pallas_call(
        _outproj_kernel,
        grid=(T // tm,),
        in_specs=[
            pl.BlockSpec((tm, D), lambda i: (i, 0)),
            pl.BlockSpec((tm, D), lambda i: (i, 0)),
            pl.BlockSpec((D, D), lambda i: (0, 0)),
            pl.BlockSpec((1, D), lambda i: (0, 0)),
        ],
        out_specs=[
            pl.BlockSpec((tm, D), lambda i: (i, 0)),
            pl.BlockSpec((D, tm), lambda i: (0, i)),
        ],
        out_shape=[
            jax.ShapeDtypeStruct((T, D), F32),
            jax.ShapeDtypeStruct((D, T), BF16),
        ],
        compiler_params=_params(("parallel",)),
        name="out_proj",
    )(x2, merged, w_bf, gain)


def _staircase():
    K = PEER_TOPK
    return [(k1, k2) for k1 in range(K) for k2 in range(K) if (k1 + 1) * (k2 + 1) <= K + 1]


def _batcher_pairs(n):
    pairs, p = [], 1
    while p < n:
        k = p
        while k >= 1:
            for j in range(k % p, n - k, 2 * k):
                for i in range(min(k, n - j - k)):
                    if (i + j) // (2 * p) == (i + j + k) // (2 * p):
                        pairs.append((i + j, i + j + k))
            k //= 2
        p *= 2
    return pairs


def _compare_exchange(v, i, j):
    v[i], v[j] = jnp.maximum(v[i], v[j]), jnp.minimum(v[i], v[j])


def _sort_desc_padded(vals):
    n = 1
    while n < len(vals):
        n *= 2
    v = list(vals) + [None] * (n - len(vals))
    for i, j in _batcher_pairs(n):
        if v[j] is None:
            continue
        if v[i] is None:
            v[i], v[j] = v[j], None
            continue
        _compare_exchange(v, i, j)
    return v


def _sorted_top16(s_ref, ls):
    H, NK, K = PEER_HEADS, PEER_N_KEYS, PEER_TOPK
    sort_net = _batcher_pairs(K)
    top = None
    for g in range(NK // K):
        v = [s_ref[(g * K + i) * H:(g * K + i + 1) * H, ls] for i in range(K)]
        for i, j in sort_net:
            _compare_exchange(v, i, j)
        if top is None:
            top = v
            continue
        top = [jnp.maximum(top[i], v[K - 1 - i]) for i in range(K)]
        k = K // 2
        while k >= 1:
            for i in range(K):
                if i & k == 0:
                    _compare_exchange(top, i, i + k)
            k //= 2
    return top


def _routing_kernel(hnt_ref, wq_ref, kbig_ref, a_ref, b_ref, th_ref, s1o, s2o, bscr):
    H, NK, K = PEER_HEADS, PEER_N_KEYS, PEER_TOPK
    tb = hnt_ref.shape[1]
    half = H * PEER_HALF_DIM
    qt = jnp.dot(wq_ref[...], hnt_ref[...], preferred_element_type=F32).astype(BF16)
    s1o[...] = jnp.dot(kbig_ref[0], qt[:half], preferred_element_type=F32)
    s2o[...] = jnp.dot(kbig_ref[1], qt[half:], preferred_element_type=F32)

    pairs = _staircase()
    for lt in range(tb // LANES):
        ls = slice(lt * LANES, (lt + 1) * LANES)
        t1v = _sorted_top16(s1o, ls)
        t2v = _sorted_top16(s2o, ls)
        cand = [t1v[k1] + t2v[k2] for (k1, k2) in pairs]
        cmax = cand[0]
        tops = _sort_desc_padded(cand)[:K + 1]
        zsum = jnp.zeros((H, LANES), F32)
        for r in range(K):
            zsum = zsum + jnp.exp(tops[r] - cmax)
        inv_z = 1.0 / zsum
        th_ref[:, ls] = jnp.exp(0.5 * (tops[K - 1] + tops[K]) - cmax) * inv_z
        m1, m2 = t1v[0], t2v[0]
        for n in range(NK):
            rows = slice(n * H, (n + 1) * H)
            s1, s2 = s1o[rows, ls], s2o[rows, ls]
            a_ref[rows, ls] = jnp.where(s1 >= t1v[K - 1], jnp.exp(s1 - m1) * inv_z, 0.0)
            bscr[lt, rows, :] = jnp.where(s2 >= t2v[K - 1], jnp.exp(s2 - m2), 0.0)
        for h in range(H):
            b_ref[h * NK:(h + 1) * NK, ls] = bscr[lt, pl.ds(h, NK, stride=H), :]


def _routing(hnt, wq_t, kbig, tb):
    D, T = hnt.shape
    R = PEER_HEADS * PEER_N_KEYS
    scr = lambda rows: pltpu.VMEM((rows, tb), F32)
    return pl.pallas_call(
        _routing_kernel,
        grid=(T // tb,),
        in_specs=[
            pl.BlockSpec((D, tb), lambda i: (0, i)),
            pl.BlockSpec(wq_t.shape, lambda i: (0, 0), pipeline_mode=pl.Buffered(1)),
            pl.BlockSpec(kbig.shape, lambda i: (0, 0, 0), pipeline_mode=pl.Buffered(1)),
        ],
        out_specs=[
            pl.BlockSpec((R, tb), lambda i: (0, i)),
            pl.BlockSpec((R, tb), lambda i: (0, i)),
            pl.BlockSpec((PEER_HEADS, tb), lambda i: (0, i)),
        ],
        out_shape=[
            jax.ShapeDtypeStruct((R, T), F32),
            jax.ShapeDtypeStruct((R, T), F32),
            jax.ShapeDtypeStruct((PEER_HEADS, T), F32),
        ],
        scratch_shapes=[scr(R), scr(R), pltpu.VMEM((tb // LANES, R, LANES), F32)],
        compiler_params=_params(("parallel",)),
        name="peer_routing",
    )(hnt, wq_t, kbig)


def _gelu(x):
    return 0.5 * x * (1.0 + lax.erf(x * (2.0 ** -0.5)))


def _experts_kernel(hnt_ref, down_ref, upt_ref, a_ref, b_ref, th_ref, yt_ref,
                    arep, threp, p_scr, *, sub, halves):
    H, NK, SL = PEER_HEADS, PEER_N_KEYS, SUBLANES
    ec, tb = down_ref.shape[0], hnt_ref.shape[1]

    @pl.when(pl.program_id(1) == 0)
    def _():
        yt_ref[...] = jnp.zeros_like(yt_ref)
        for h in range(H):
            threp[h] = jnp.broadcast_to(th_ref[h:h + 1, :], (SL, tb))

    for r in range(ec // NK * H):
        arep[r] = jnp.broadcast_to(a_ref[r:r + 1, :], (SL, tb))

    hnt = hnt_ref[...]
    n_sub = ec // sub
    for s in range(n_sub):
        at = jnp.dot(down_ref[s * sub:(s + 1) * sub, :], hnt, preferred_element_type=F32)
        for c in range(sub // NK):
            i1 = s * (sub // NK) + c
            for lt in range(tb // LANES):
                ls = slice(lt * LANES, (lt + 1) * LANES)
                for g2 in range(NK // (2 * SL)):
                    tiles = []
                    for g in (2 * g2, 2 * g2 + 1):
                        w = jnp.zeros((SL, LANES), F32)
                        for h in range(H):
                            prod = arep[i1 * H + h, :, ls] * b_ref[h * NK + g * SL:h * NK + (g + 1) * SL, ls]
                            w = w + jnp.where(prod >= threp[h, :, ls], prod, 0.0)
                        r0 = c * NK + g * SL
                        tiles.append(w * _gelu(at[r0:r0 + SL, ls]))
                    r0 = s * sub + c * NK + 2 * g2 * SL
                    p_scr[r0:r0 + 2 * SL, ls] = jnp.concatenate(tiles, axis=0).astype(BF16)
        if (s + 1) % (n_sub // halves) == 0:
            k0, k1 = (s + 1 - n_sub // halves) * sub, (s + 1) * sub
            yt_ref[...] += jnp.dot(upt_ref[:, k0:k1], p_scr[k0:k1, :], preferred_element_type=F32)


def _experts(hnt, down_bf, up, a_t, b_t, th_t, tb, ec, sub=512, halves=1):
    D, T = hnt.shape
    NE = down_bf.shape[0]
    a_rows = ec // PEER_N_KEYS * PEER_HEADS
    upt_bf = up.reshape(NE // ec, ec, D).transpose(0, 2, 1).astype(BF16)
    return pl.pallas_call(
        functools.partial(_experts_kernel, sub=sub, halves=halves),
        grid=(T // tb, NE // ec),
        in_specs=[
            pl.BlockSpec((D, tb), lambda t, e: (0, t), pipeline_mode=pl.Buffered(1)),
            pl.BlockSpec((ec, D), lambda t, e: (e, 0)),
            pl.BlockSpec((None, D, ec), lambda t, e: (e, 0, 0)),
            pl.BlockSpec((a_rows, tb), lambda t, e: (e, t)),
            pl.BlockSpec((b_t.shape[0], tb), lambda t, e: (0, t), pipeline_mode=pl.Buffered(1)),
            pl.BlockSpec((PEER_HEADS, tb), lambda t, e: (0, t)),
        ],
        out_specs=pl.BlockSpec((D, tb), lambda t, e: (0, t)),
        out_shape=jax.ShapeDtypeStruct((D, T), F32),
        scratch_shapes=[pltpu.VMEM((a_rows, SUBLANES, tb), F32),
                        pltpu.VMEM((PEER_HEADS, SUBLANES, tb), F32),
                        pltpu.VMEM((ec, tb), BF16)],
        compiler_params=_params(("parallel", "arbitrary"), EXPERTS_VMEM_LIMIT),
        name="peer_experts",
    )(hnt, down_bf, upt_bf, a_t, b_t, th_t)


def _finish_kernel(h_ref, yt_ref, g_ref, o_ref):
    h = h_ref[...] + yt_ref[...].T
    ms = jnp.mean(h * h, axis=-1, keepdims=True)
    o_ref[...] = h * lax.rsqrt(ms + NORM_EPS) * g_ref[...]


def _finish(h, yt, gain, tm):
    T, D = h.shape
    return pl.pallas_call(
        _finish_kernel,
        grid=(T // tm,),
        in_specs=[
            pl.BlockSpec((tm, D), lambda i: (i, 0)),
            pl.BlockSpec((D, tm), lambda i: (0, i)),
            pl.BlockSpec((1, D), lambda i: (0, 0)),
        ],
        out_specs=pl.BlockSpec((tm, D), lambda i: (i, 0)),
        out_shape=jax.ShapeDtypeStruct((T, D), F32),
        compiler_params=_params(("parallel",)),
        name="finish",
    )(h, yt, gain)


def _w_in_block_table(tn):
    src_off = {"qa": 0, "kv": 1024, "qr": 1536, "kr": 2560, "vr": 3584, "gr": 5632, "ga": 7680, "grt": 9728}
    width = {"qa": 1024, "kv": 512, "qr": 1024, "kr": 1024, "vr": 2048, "gr": 2048, "ga": 2048, "grt": 2048}
    table = []
    for name in ("vr", "gr", "ga", "grt", "qa", "qr", "kr", "kv"):
        assert src_off[name] % tn == 0 and width[name] % tn == 0
        table += [src_off[name] // tn + i for i in range(width[name] // tn)]
    return table


def _pack_kernel(w_ref, o_ref):
    o_ref[...] = w_ref[...].astype(o_ref.dtype)


def _pack_w_in(w, tn):
    D, W = w.shape
    table = _w_in_block_table(tn)

    def src_block(j):
        blk = jnp.int32(0)
        for k, t in enumerate(table):
            blk = jnp.where(j == k, t, blk)
        return blk

    return pl.pallas_call(
        _pack_kernel,
        grid=(W // tn,),
        in_specs=[pl.BlockSpec((D, tn), lambda j: (0, src_block(j)))],
        out_specs=pl.BlockSpec((None, D, tn), lambda j: (j, 0, 0)),
        out_shape=jax.ShapeDtypeStruct((W // tn, D, tn), BF16),
        compiler_params=_params(("parallel",)),
        name="pack_w_in",
    )(w)


def _rope_tables(S):
    pos = jnp.arange(S, dtype=F32)
    inv_freq = 1.0 / (ROPE_THETA ** (jnp.arange(0, HEAD_DIM, 2, dtype=F32) / HEAD_DIM))
    ang = pos[:, None] * inv_freq[None, :]
    cos, sin = jnp.cos(ang), jnp.sin(ang)
    return jnp.concatenate([cos, cos], axis=1), jnp.concatenate([-sin, sin], axis=1)


def kernel(x, attn_norm, w_in, attn_sinks, w_attn_branch, w_ret_branch, w_out, ffn_norm,
           w_peer_query, peer_sub_keys, peer_expert_down, peer_expert_up, final_norm):
    B, S, D = x.shape
    T = B * S
    assert w_in.shape[0] == 1, "single-layer block"
    assert S % CHUNK == 0 and T % 512 == 0

    x2 = x.reshape(T, D)
    cos_full, sin_signed = _rope_tables(S)

    proj = _inproj(x2, attn_norm[0][None, :], _pack_w_in(w_in[0], tn=512), tm=1024)
    y_a = _attention(proj, attn_sinks[0], cos_full, sin_signed, B, S)
    y_r = _retention(proj, cos_full, sin_signed, B, S)
    merged = _merge(y_a, y_r, w_attn_branch[0].astype(BF16), w_ret_branch[0].astype(BF16),
                    proj, tm=512, tn=512)
    h, hnt = _outproj(x2, merged, w_out[0].astype(BF16), ffn_norm[0][None, :], tm=256)

    H, NK, HD = PEER_HEADS, PEER_N_KEYS, PEER_HALF_DIM
    wq_t = w_peer_query[0].reshape(D, H, 2, HD).transpose(2, 1, 3, 0).reshape(2 * H * HD, D).astype(BF16)
    eye = jnp.eye(H, dtype=F32)
    kbig = jnp.einsum("phnd,hg->pnhgd", peer_sub_keys[0], eye).reshape(2, NK * H, H * HD).astype(BF16)
    a_t, b_t, th_t = _routing(hnt, wq_t, kbig, tb=512)

    yt = _experts(hnt, peer_expert_down[0].astype(BF16), peer_expert_up[0],
                  a_t, b_t, th_t, tb=1024, ec=1024)
    out = _finish(h, yt, final_norm[None, :], tm=256)
    return out.reshape(B, S, D)
```

```python
import functools

import jax
import jax.numpy as jnp
from jax import lax
from jax.experimental import pallas as pl
from jax.experimental.pallas import tpu as pltpu

F32 = jnp.float32
BF16 = jnp.bfloat16

HEAD_DIM = 128
ATTN_Q_HEADS = 8
ATTN_KV_HEADS = 2
ATTN_GROUP = ATTN_Q_HEADS // ATTN_KV_HEADS
WINDOW = 128
CHUNK = 128
ROPE_THETA = 10000.0
RET_HEADS = 8
RET_QK_DIM = 128
RET_V_DIM = 256
PEER_HEADS = 8
PEER_N_KEYS = 128
PEER_TOPK = 16
PEER_HALF_DIM = 128
NORM_EPS = 1e-6
NEG_INF = float("-inf")

LANES = 128
SUBLANES = 8
VMEM_LIMIT = 48 * 1024 * 1024
EXPERTS_VMEM_LIMIT = 58 * 1024 * 1024

SEG_W2048 = 2048
COL_VR, COL_GR, COL_GA, COL_GRT = 0, 2048, 4096, 6144
COL_QA, COL_QR, COL_KR = 8192, 9216, 10240
COL_KA, COL_VA = 11264, 11520
IN_WIDTH = 11776


def _params(sem, vmem_limit=VMEM_LIMIT):
    return pltpu.CompilerParams(dimension_semantics=sem, vmem_limit_bytes=vmem_limit)


def _rope(x, cos_full, sin_signed):
    return x * cos_full + pltpu.roll(x, HEAD_DIM // 2, axis=1) * sin_signed


def _inproj_kernel(x_ref, g_ref, w_ref, o_ref, xn_ref):
    @pl.when(pl.program_id(1) == 0)
    def _():
        x = x_ref[...]
        ms = jnp.mean(x * x, axis=-1, keepdims=True)
        xn_ref[...] = (x * lax.rsqrt(ms + NORM_EPS) * g_ref[...]).astype(BF16)

    o_ref[...] = jnp.dot(xn_ref[...], w_ref[...], preferred_element_type=F32).astype(o_ref.dtype)


def _inproj(x2, gain, w_blocks, tm):
    T, D = x2.shape
    nb, _, tn = w_blocks.shape
    W = nb * tn
    return pl.pallas_call(
        _inproj_kernel,
        grid=(T // tm, W // tn),
        in_specs=[
            pl.BlockSpec((tm, D), lambda i, j: (i, 0)),
            pl.BlockSpec((1, D), lambda i, j: (0, 0)),
            pl.BlockSpec((None, D, tn), lambda i, j: (j, 0, 0)),
        ],
        out_specs=pl.BlockSpec((tm, tn), lambda i, j: (i, j)),
        out_shape=jax.ShapeDtypeStruct((T, W), BF16),
        scratch_shapes=[pltpu.VMEM((tm, D), BF16)],
        compiler_params=_params(("parallel", "arbitrary")),
        name="in_proj",
    )(x2, gain, w_blocks)


def _attn_kernel(sinks_ref, q_ref, kc_ref, kp_ref, vc_ref, vp_ref,
                 cos_ref, sin_ref, cosp_ref, sinp_ref, o_ref):
    n = pl.program_id(1)
    C = CHUNK
    cos, sin = cos_ref[...], sin_ref[...]
    cosp, sinp = cosp_ref[...], sinp_ref[...]
    qi = lax.broadcasted_iota(jnp.int32, (C, 2 * C), 0)
    kj = lax.broadcasted_iota(jnp.int32, (C, 2 * C), 1)
    diff = qi + C - kj
    in_band = jnp.where(diff >= 0, jnp.where(diff < WINDOW, 1, 0), 0)
    has_prev = jnp.where(n > 0, 1, 0)
    allowed = in_band * jnp.where(kj >= C, 1, has_prev) > 0

    for hk in range(ATTN_KV_HEADS):
        ksl = slice(hk * HEAD_DIM, (hk + 1) * HEAD_DIM)
        kc = _rope(kc_ref[:, ksl].astype(F32), cos, sin)
        kp = _rope(kp_ref[:, ksl].astype(F32), cosp, sinp)
        kb = jnp.concatenate([kp, kc], axis=0).astype(BF16)
        vb = jnp.concatenate([vp_ref[:, ksl], vc_ref[:, ksl]], axis=0)
        for g in range(ATTN_GROUP):
            h = hk * ATTN_GROUP + g
            hsl = slice(h * HEAD_DIM, (h + 1) * HEAD_DIM)
            q = _rope(q_ref[:, hsl].astype(F32), cos, sin).astype(BF16)
            s = lax.dot_general(q, kb, (((1,), (1,)), ((), ())), preferred_element_type=F32)
            s = jnp.where(allowed, s * (HEAD_DIM ** -0.5), -1e30)
            sink = sinks_ref[h]
            m = jnp.maximum(jnp.max(s, axis=-1, keepdims=True), sink)
            p = jnp.exp(s - m)
            denom = jnp.sum(p, axis=-1, keepdims=True) + jnp.exp(sink - m)
            o = jnp.dot(p.astype(BF16), vb, preferred_element_type=F32)
            o_ref[:, hsl] = (o / denom).astype(o_ref.dtype)


def _attention(proj, sinks, cos_full, sin_signed, B, S):
    C = CHUNK
    N = S // C
    qa_blk = COL_QA // (ATTN_Q_HEADS * HEAD_DIM)
    ka_blk = COL_KA // (ATTN_KV_HEADS * HEAD_DIM)
    va_blk = COL_VA // (ATTN_KV_HEADS * HEAD_DIM)
    kvw = ATTN_KV_HEADS * HEAD_DIM
    cur = lambda b, n: b * N + n
    prev = lambda b, n: b * N + jnp.maximum(n - 1, 0)
    return pl.pallas_call(
        _attn_kernel,
        grid=(B, N),
        in_specs=[
            pl.BlockSpec(memory_space=pltpu.SMEM),
            pl.BlockSpec((C, ATTN_Q_HEADS * HEAD_DIM), lambda b, n: (cur(b, n), qa_blk)),
            pl.BlockSpec((C, kvw), lambda b, n: (cur(b, n), ka_blk)),
            pl.BlockSpec((C, kvw), lambda b, n: (prev(b, n), ka_blk)),
            pl.BlockSpec((C, kvw), lambda b, n: (cur(b, n), va_blk)),
            pl.BlockSpec((C, kvw), lambda b, n: (prev(b, n), va_blk)),
            pl.BlockSpec((C, HEAD_DIM), lambda b, n: (n, 0)),
            pl.BlockSpec((C, HEAD_DIM), lambda b, n: (n, 0)),
            pl.BlockSpec((C, HEAD_DIM), lambda b, n: (jnp.maximum(n - 1, 0), 0)),
            pl.BlockSpec((C, HEAD_DIM), lambda b, n: (jnp.maximum(n - 1, 0), 0)),
        ],
        out_specs=pl.BlockSpec((C, ATTN_Q_HEADS * HEAD_DIM), lambda b, n: (cur(b, n), 0)),
        out_shape=jax.ShapeDtypeStruct((B * S, ATTN_Q_HEADS * HEAD_DIM), BF16),
        compiler_params=_params(("parallel", "arbitrary")),
        name="swa_attention",
    )(sinks, proj, proj, proj, proj, proj, cos_full, sin_signed, cos_full, sin_signed)


def _ret_kernel(cdec_ref, q_ref, k_ref, v_ref, g_ref, cos_ref, sin_ref,
                dmask_ref, qdec_ref, kdec_ref, o_ref, state_ref):
    @pl.when(pl.program_id(1) == 0)
    def _():
        state_ref[...] = jnp.zeros_like(state_ref)

    cos, sin = cos_ref[...], sin_ref[...]
    for h in range(RET_HEADS):
        qsl = slice(h * RET_QK_DIM, (h + 1) * RET_QK_DIM)
        vsl = slice(h * RET_V_DIM, (h + 1) * RET_V_DIM)
        q = _rope(q_ref[:, qsl].astype(F32), cos, sin)
        k = _rope(k_ref[:, qsl].astype(F32), cos, sin) * (RET_QK_DIM ** -0.5)
        v = v_ref[:, vsl]
        inner = lax.dot_general(q.astype(BF16), k.astype(BF16), (((1,), (1,)), ((), ())),
                                preferred_element_type=F32) * dmask_ref[h]
        o = jnp.dot(inner.astype(BF16), v, preferred_element_type=F32)
        state = state_ref[h]
        o = o + jnp.dot((q * qdec_ref[h]).astype(BF16), state.astype(BF16),
                        preferred_element_type=F32)
        kd_t = (k * kdec_ref[h]).T.astype(BF16)
        state_ref[h] = state * cdec_ref[h] + jnp.dot(kd_t, v, preferred_element_type=F32)
        mu = jnp.mean(o, axis=-1, keepdims=True)
        oc = o - mu
        var = jnp.mean(oc * oc, axis=-1, keepdims=True)
        gate = g_ref[:, vsl].astype(F32)
        o_ref[:, vsl] = (oc * lax.rsqrt(var + NORM_EPS) * (gate * jax.nn.sigmoid(gate))).astype(o_ref.dtype)


def _retention(proj, cos_full, sin_signed, B, S):
    C = CHUNK
    N = S // C
    H = RET_HEADS
    log_gamma = jnp.log(1.0 - 2.0 ** (-5.0 - jnp.arange(H, dtype=F32)))
    pos = jnp.arange(C, dtype=F32)
    d = pos[:, None] - pos[None, :]
    dmask = jnp.where(d >= 0, jnp.exp(log_gamma[:, None, None] * jnp.maximum(d, 0.0)), 0.0)
    kdec = jnp.exp(log_gamma[:, None] * (C - 1.0 - pos)[None, :])
    qdec = jnp.exp(log_gamma[:, None] * (pos + 1.0)[None, :])
    cdec = jnp.exp(log_gamma * C)
    qdec_b = jnp.broadcast_to(qdec[:, :, None], (H, C, RET_QK_DIM))
    kdec_b = jnp.broadcast_to(kdec[:, :, None], (H, C, RET_QK_DIM))
    qkw, vw = H * RET_QK_DIM, H * RET_V_DIM
    row = lambda b, n: b * N + n
    full3 = lambda b, n: (0, 0, 0)
    return pl.pallas_call(
        _ret_kernel,
        grid=(B, N),
        in_specs=[
            pl.BlockSpec(memory_space=pltpu.SMEM),
            pl.BlockSpec((C, qkw), lambda b, n: (row(b, n), COL_QR // qkw)),
            pl.BlockSpec((C, qkw), lambda b, n: (row(b, n), COL_KR // qkw)),
            pl.BlockSpec((C, vw), lambda b, n: (row(b, n), COL_VR // vw)),
            pl.BlockSpec((C, vw), lambda b, n: (row(b, n), COL_GR // vw)),
            pl.BlockSpec((C, HEAD_DIM), lambda b, n: (n, 0)),
            pl.BlockSpec((C, HEAD_DIM), lambda b, n: (n, 0)),
            pl.BlockSpec((H, C, C), full3),
            pl.BlockSpec((H, C, RET_QK_DIM), full3),
            pl.BlockSpec((H, C, RET_QK_DIM), full3),
        ],
        out_specs=pl.BlockSpec((C, vw), lambda b, n: (row(b, n), 0)),
        out_shape=jax.ShapeDtypeStruct((B * S, vw), BF16),
        scratch_shapes=[pltpu.VMEM((H, RET_QK_DIM, RET_V_DIM), F32)],
        compiler_params=_params(("parallel", "arbitrary")),
        name="retention",
    )(cdec, proj, proj, proj, proj, cos_full, sin_signed, dmask, qdec_b, kdec_b)


def _merge_kernel(ya_ref, yr_ref, wa_ref, wr_ref, ga_ref, gr_ref, o_ref):
    pa = jnp.dot(ya_ref[...], wa_ref[...], preferred_element_type=F32)
    pr = jnp.dot(yr_ref[...], wr_ref[...], preferred_element_type=F32)
    ga = jax.nn.sigmoid(ga_ref[...].astype(F32))
    gr = jax.nn.sigmoid(gr_ref[...].astype(F32))
    o_ref[...] = (ga * pa + gr * pr).astype(o_ref.dtype)


def _merge(ya, yr, wa_bf, wr_bf, proj, tm, tn):
    T = ya.shape[0]
    D = wa_bf.shape[1]
    return pl.pallas_call(
        _merge_kernel,
        grid=(T // tm, D // tn),
        in_specs=[
            pl.BlockSpec((tm, ya.shape[1]), lambda i, j: (i, 0)),
            pl.BlockSpec((tm, yr.shape[1]), lambda i, j: (i, 0)),
            pl.BlockSpec((wa_bf.shape[0], tn), lambda i, j: (0, j)),
            pl.BlockSpec((wr_bf.shape[0], tn), lambda i, j: (0, j)),
            pl.BlockSpec((tm, tn), lambda i, j: (i, COL_GA // tn + j)),
            pl.BlockSpec((tm, tn), lambda i, j: (i, COL_GRT // tn + j)),
        ],
        out_specs=pl.BlockSpec((tm, tn), lambda i, j: (i, j)),
        out_shape=jax.ShapeDtypeStruct((T, D), BF16),
        compiler_params=_params(("parallel", "arbitrary")),
        name="merge_branches",
    )(ya, yr, wa_bf, wr_bf, proj, proj)


def _outproj_kernel(x_ref, m_ref, w_ref, g_ref, h_ref, hnt_ref):
    h = x_ref[...] + jnp.dot(m_ref[...], w_ref[...], preferred_element_type=F32)
    h_ref[...] = h
    ms = jnp.mean(h * h, axis=-1, keepdims=True)
    hn = h * lax.rsqrt(ms + NORM_EPS) * g_ref[...]
    hnt_ref[...] = hn.T.astype(hnt_ref.dtype)


def _outproj(x2, merged, w_bf, gain, tm):
    T, D = x2.shape
    return pl.pallas_call(
        _outproj_kernel,
        grid=(T // tm,),
        in_specs=[
            pl.BlockSpec((tm, D), lambda i: (i, 0)),
            pl.BlockSpec((tm, D), lambda i: (i, 0)),
            pl.BlockSpec((D, D), lambda i: (0, 0)),
            pl.BlockSpec((1, D), lambda i: (0, 0)),
        ],
        out_specs=[
            pl.BlockSpec((tm, D), lambda i: (i, 0)),
            pl.BlockSpec((D, tm), lambda i: (0, i)),
        ],
        out_shape=[
            jax.ShapeDtypeStruct((T, D), F32),
            jax.ShapeDtypeStruct((D, T), BF16),
        ],
        compiler_params=_params(("parallel",)),
        name="out_proj",
    )(x2, merged, w_bf, gain)


def _staircase():
    K = PEER_TOPK
    return [(k1, k2) for k1 in range(K) for k2 in range(K) if (k1 + 1) * (k2 + 1) <= K + 1]


def _batcher_pairs(n):
    pairs, p = [], 1
    while p < n:
        k = p
        while k >= 1:
            for j in range(k % p, n - k, 2 * k):
                for i in range(min(k, n - j - k)):
                    if (i + j) // (2 * p) == (i + j + k) // (2 * p):
                        pairs.append((i + j, i + j + k))
            k //= 2
        p *= 2
    return pairs


def _compare_exchange(v, i, j):
    v[i], v[j] = jnp.maximum(v[i], v[j]), jnp.minimum(v[i], v[j])


def _sort_desc_padded(vals):
    n = 1
    while n < len(vals):
        n *= 2
    v = list(vals) + [None] * (n - len(vals))
    for i, j in _batcher_pairs(n):
        if v[j] is None:
            continue
        if v[i] is None:
            v[i], v[j] = v[j], None
            continue
        _compare_exchange(v, i, j)
    return v


def _sorted_top16(s_ref, ls):
    H, NK, K = PEER_HEADS, PEER_N_KEYS, PEER_TOPK
    sort_net = _batcher_pairs(K)
    top = None
    for g in range(NK // K):
        v = [s_ref[(g * K + i) * H:(g * K + i + 1) * H, ls] for i in range(K)]
        for i, j in sort_net:
            _compare_exchange(v, i, j)
        if top is None:
            top = v
            continue
        top = [jnp.maximum(top[i], v[K - 1 - i]) for i in range(K)]
        k = K // 2
        while k >= 1:
            for i in range(K):
                if i & k == 0:
                    _compare_exchange(top, i, i + k)
            k //= 2
    return top


def _routing_kernel(hnt_ref, wq_ref, kbig_ref, a_ref, b_ref, th_ref, s1o, s2o, bscr):
    H, NK, K = PEER_HEADS, PEER_N_KEYS, PEER_TOPK
    tb = hnt_ref.shape[1]
    half = H * PEER_HALF_DIM
    qt = jnp.dot(wq_ref[...], hnt_ref[...], preferred_element_type=F32).astype(BF16)
    s1o[...] = jnp.dot(kbig_ref[0], qt[:half], preferred_element_type=F32)
    s2o[...] = jnp.dot(kbig_ref[1], qt[half:], preferred_element_type=F32)

    pairs = _staircase()
    for lt in range(tb // LANES):
        ls = slice(lt * LANES, (lt + 1) * LANES)
        t1v = _sorted_top16(s1o, ls)
        t2v = _sorted_top16(s2o, ls)
        cand = [t1v[k1] + t2v[k2] for (k1, k2) in pairs]
        cmax = cand[0]
        tops = _sort_desc_padded(cand)[:K + 1]
        zsum = jnp.zeros((H, LANES), F32)
        for r in range(K):
            zsum = zsum + jnp.exp(tops[r] - cmax)
        inv_z = 1.0 / zsum
        th_ref[:, ls] = jnp.exp(0.5 * (tops[K - 1] + tops[K]) - cmax) * inv_z
        m1, m2 = t1v[0], t2v[0]
        for n in range(NK):
            rows = slice(n * H, (n + 1) * H)
            s1, s2 = s1o[rows, ls], s2o[rows, ls]
            a_ref[rows, ls] = jnp.where(s1 >= t1v[K - 1], jnp.exp(s1 - m1) * inv_z, 0.0)
            bscr[lt, rows, :] = jnp.where(s2 >= t2v[K - 1], jnp.exp(s2 - m2), 0.0)
        for h in range(H):
            b_ref[h * NK:(h + 1) * NK, ls] = bscr[lt, pl.ds(h, NK, stride=H), :]


def _routing(hnt, wq_t, kbig, tb):
    D, T = hnt.shape
    R = PEER_HEADS * PEER_N_KEYS
    scr = lambda rows: pltpu.VMEM((rows, tb), F32)
    return pl.pallas_call(
        _routing_kernel,
        grid=(T // tb,),
        in_specs=[
            pl.BlockSpec((D, tb), lambda i: (0, i)),
            pl.BlockSpec(wq_t.shape, lambda i: (0, 0), pipeline_mode=pl.Buffered(1)),
            pl.BlockSpec(kbig.shape, lambda i: (0, 0, 0), pipeline_mode=pl.Buffered(1)),
        ],
        out_specs=[
            pl.BlockSpec((R, tb), lambda i: (0, i)),
            pl.BlockSpec((R, tb), lambda i: (0, i)),
            pl.BlockSpec((PEER_HEADS, tb), lambda i: (0, i)),
        ],
        out_shape=[
            jax.ShapeDtypeStruct((R, T), F32),
            jax.ShapeDtypeStruct((R, T), F32),
            jax.ShapeDtypeStruct((PEER_HEADS, T), F32),
        ],
        scratch_shapes=[scr(R), scr(R), pltpu.VMEM((tb // LANES, R, LANES), F32)],
        compiler_params=_params(("parallel",)),
        name="peer_routing",
    )(hnt, wq_t, kbig)


def _gelu(x):
    return 0.5 * x * (1.0 + lax.erf(x * (2.0 ** -0.5)))


def _experts_kernel(hnt_ref, down_ref, upt_ref, a_ref, b_ref, th_ref, yt_ref,
                    arep, threp, p_scr, at_scr, *, sub, piece):
    H, NK, SL = PEER_HEADS, PEER_N_KEYS, SUBLANES
    ec, tb = down_ref.shape[0], hnt_ref.shape[1]

    @pl.when(pl.program_id(1) == 0)
    def _():
        yt_ref[...] = jnp.zeros_like(yt_ref)
        for h in range(H):
            threp[h] = jnp.broadcast_to(th_ref[h:h + 1, :], (SL, tb))

    for r in range(ec // NK * H):
        arep[r] = jnp.broadcast_to(a_ref[r:r + 1, :], (SL, tb))

    n_sub = ec // sub
    n_piece = tb // piece

    def down_proj(n):
        cols = slice(n * piece, (n + 1) * piece)
        for s in range(n_sub):
            at_scr[n % 2, s * sub:(s + 1) * sub, :] = jnp.dot(
                down_ref[s * sub:(s + 1) * sub, :], hnt_ref[:, cols], preferred_element_type=F32)

    def up_proj(n):
        cols = slice(n * piece, (n + 1) * piece)
        yt_ref[:, cols] += jnp.dot(upt_ref[...], p_scr[:, cols], preferred_element_type=F32)

    def gate_build(n):
        for s in range(n_sub):
            for c in range(sub // NK):
                i1 = s * (sub // NK) + c
                for lt in range(piece // LANES):
                    ls = slice(n * piece + lt * LANES, n * piece + (lt + 1) * LANES)
                    lp = slice(lt * LANES, (lt + 1) * LANES)
                    for g2 in range(NK // (2 * SL)):
                        tiles = []
                        for g in (2 * g2, 2 * g2 + 1):
                            w = jnp.zeros((SL, LANES), F32)
                            for h in range(H):
                                prod = arep[i1 * H + h, :, ls] * b_ref[h * NK + g * SL:h * NK + (g + 1) * SL, ls]
                                w = w + jnp.where(prod >= threp[h, :, ls], prod, 0.0)
                            r0 = s * sub + c * NK + g * SL
                            tiles.append(w * _gelu(at_scr[n % 2, r0:r0 + SL, lp]))
                        r0 = s * sub + c * NK + 2 * g2 * SL
                        p_scr[r0:r0 + 2 * SL, ls] = jnp.concatenate(tiles, axis=0).astype(BF16)

    down_proj(0)
    for n in range(n_piece):
        if n + 1 < n_piece:
            down_proj(n + 1)
        if n >= 1:
            up_proj(n - 1)
        gate_build(n)
    up_proj(n_piece - 1)


def _experts(hnt, down_bf, up, a_t, b_t, th_t, tb, ec, sub=512, piece=256):
    D, T = hnt.shape
    NE = down_bf.shape[0]
    a_rows = ec // PEER_N_KEYS * PEER_HEADS
    upt_bf = up.reshape(NE // ec, ec, D).transpose(0, 2, 1).astype(BF16)
    return pl.pallas_call(
        functools.partial(_experts_kernel, sub=sub, piece=piece),
        grid=(T // tb, NE // ec),
        in_specs=[
            pl.BlockSpec((D, tb), lambda t, e: (0, t), pipeline_mode=pl.Buffered(1)),
            pl.BlockSpec((ec, D), lambda t, e: (e, 0)),
            pl.BlockSpec((None, D, ec), lambda t, e: (e, 0, 0)),
            pl.BlockSpec((a_rows, tb), lambda t, e: (e, t)),
            pl.BlockSpec((b_t.shape[0], tb), lambda t, e: (0, t), pipeline_mode=pl.Buffered(1)),
            pl.BlockSpec((PEER_HEADS, tb), lambda t, e: (0, t)),
        ],
        out_specs=pl.BlockSpec((D, tb), lambda t, e: (0, t)),
        out_shape=jax.ShapeDtypeStruct((D, T), F32),
        scratch_shapes=[pltpu.VMEM((a_rows, SUBLANES, tb), F32),
                        pltpu.VMEM((PEER_HEADS, SUBLANES, tb), F32),
                        pltpu.VMEM((ec, tb), BF16),
                        pltpu.VMEM((2, ec, piece), F32)],
        compiler_params=_params(("parallel", "arbitrary"), EXPERTS_VMEM_LIMIT),
        name="peer_experts",
    )(hnt, down_bf, upt_bf, a_t, b_t, th_t)


def _finish_kernel(h_ref, yt_ref, g_ref, o_ref):
    h = h_ref[...] + yt_ref[...].T
    ms = jnp.mean(h * h, axis=-1, keepdims=True)
    o_ref[...] = h * lax.rsqrt(ms + NORM_EPS) * g_ref[...]


def _finish(h, yt, gain, tm):
    T, D = h.shape
    return pl.pallas_call(
        _finish_kernel,
        grid=(T // tm,),
        in_specs=[
            pl.BlockSpec((tm, D), lambda i: (i, 0)),
            pl.BlockSpec((D, tm), lambda i: (0, i)),
            pl.BlockSpec((1, D), lambda i: (0, 0)),
        ],
        out_specs=pl.BlockSpec((tm, D), lambda i: (i, 0)),
        out_shape=jax.ShapeDtypeStruct((T, D), F32),
        compiler_params=_params(("parallel",)),
        name="finish",
    )(h, yt, gain)


def _w_in_block_table(tn):
    src_off = {"qa": 0, "kv": 1024, "qr": 1536, "kr": 2560, "vr": 3584, "gr": 5632, "ga": 7680, "grt": 9728}
    width = {"qa": 1024, "kv": 512, "qr": 1024, "kr": 1024, "vr": 2048, "gr": 2048, "ga": 2048, "grt": 2048}
    table = []
    for name in ("vr", "gr", "ga", "grt", "qa", "qr", "kr", "kv"):
        assert src_off[name] % tn == 0 and width[name] % tn == 0
        table += [src_off[name] // tn + i for i in range(width[name] // tn)]
    return table


def _pack_kernel(w_ref, o_ref):
    o_ref[...] = w_ref[...].astype(o_ref.dtype)


def _pack_w_in(w, tn):
    D, W = w.shape
    table = _w_in_block_table(tn)

    def src_block(j):
        blk = jnp.int32(0)
        for k, t in enumerate(table):
            blk = jnp.where(j == k, t, blk)
        return blk

    return pl.pallas_call(
        _pack_kernel,
        grid=(W // tn,),
        in_specs=[pl.BlockSpec((D, tn), lambda j: (0, src_block(j)))],
        out_specs=pl.BlockSpec((None, D, tn), lambda j: (j, 0, 0)),
        out_shape=jax.ShapeDtypeStruct((W // tn, D, tn), BF16),
        compiler_params=_params(("parallel",)),
        name="pack_w_in",
    )(w)


def _rope_tables(S):
    pos = jnp.arange(S, dtype=F32)
    inv_freq = 1.0 / (ROPE_THETA ** (jnp.arange(0, HEAD_DIM, 2, dtype=F32) / HEAD_DIM))
    ang = pos[:, None] * inv_freq[None, :]
    cos, sin = jnp.cos(ang), jnp.sin(ang)
    return jnp.concatenate([cos, cos], axis=1), jnp.concatenate([-sin, sin], axis=1)


def kernel(x, attn_norm, w_in, attn_sinks, w_attn_branch, w_ret_branch, w_out, ffn_norm,
           w_peer_query, peer_sub_keys, peer_expert_down, peer_expert_up, final_norm):
    B, S, D = x.shape
    T = B * S
    assert w_in.shape[0] == 1, "single-layer block"
    assert S % CHUNK == 0 and T % 512 == 0

    x2 = x.reshape(T, D)
    cos_full, sin_signed = _rope_tables(S)

    proj = _inproj(x2, attn_norm[0][None, :], _pack_w_in(w_in[0], tn=512), tm=1024)
    y_a = _attention(proj, attn_sinks[0], cos_full, sin_signed, B, S)
    y_r = _retention(proj, cos_full, sin_signed, B, S)
    merged = _merge(y_a, y_r, w_attn_branch[0].astype(BF16), w_ret_branch[0].astype(BF16),
                    proj, tm=1024, tn=512)
    h, hnt = _outproj(x2, merged, w_out[0].astype(BF16), ffn_norm[0][None, :], tm=256)

    H, NK, HD = PEER_HEADS, PEER_N_KEYS, PEER_HALF_DIM
    wq_t = w_peer_query[0].reshape(D, H, 2, HD).transpose(2, 1, 3, 0).reshape(2 * H * HD, D).astype(BF16)
    eye = jnp.eye(H, dtype=F32)
    kbig = jnp.einsum("phnd,hg->pnhgd", peer_sub_keys[0], eye).reshape(2, NK * H, H * HD).astype(BF16)
    a_t, b_t, th_t = _routing(hnt, wq_t, kbig, tb=512)

    yt = _experts(hnt, peer_expert_down[0].astype(BF16), peer_expert_up[0],
                  a_t, b_t, th_t, tb=1024, ec=1024)
    out = _finish(h, yt, final_norm[None, :], tm=256)
    return out.reshape(B, S, D)
```

```python
import functools

import jax
import jax.numpy as jnp
from jax import lax
from jax.experimental import pallas as pl
from jax.experimental.pallas import tpu as pltpu

F32 = jnp.float32
BF16 = jnp.bfloat16

HEAD_DIM = 128
ATTN_Q_HEADS = 8
ATTN_KV_HEADS = 2
ATTN_GROUP = ATTN_Q_HEADS // ATTN_KV_HEADS
WINDOW = 128
CHUNK = 128
ROPE_THETA = 10000.0
RET_HEADS = 8
RET_QK_DIM = 128
RET_V_DIM = 256
PEER_HEADS = 8
PEER_N_KEYS = 128
PEER_TOPK = 16
PEER_HALF_DIM = 128
NORM_EPS = 1e-6
NEG_INF = float("-inf")

LANES = 128
SUBLANES = 8
VMEM_LIMIT = 48 * 1024 * 1024
EXPERTS_VMEM_LIMIT = 58 * 1024 * 1024

SEG_W2048 = 2048
COL_VR, COL_GR, COL_GA, COL_GRT = 0, 2048, 4096, 6144
COL_QA, COL_QR, COL_KR = 8192, 9216, 10240
COL_KA, COL_VA = 11264, 11520
IN_WIDTH = 11776


def _params(sem, vmem_limit=VMEM_LIMIT):
    return pltpu.CompilerParams(dimension_semantics=sem, vmem_limit_bytes=vmem_limit)


def _rope(x, cos_full, sin_signed):
    return x * cos_full + pltpu.roll(x, HEAD_DIM // 2, axis=1) * sin_signed


def _inproj_kernel(x_ref, g_ref, w_ref, o_ref, xn_ref):
    @pl.when(pl.program_id(1) == 0)
    def _():
        x = x_ref[...]
        ms = jnp.mean(x * x, axis=-1, keepdims=True)
        xn_ref[...] = (x * lax.rsqrt(ms + NORM_EPS) * g_ref[...]).astype(BF16)

    o_ref[...] = jnp.dot(xn_ref[...], w_ref[...], preferred_element_type=F32).astype(o_ref.dtype)


def _inproj(x2, gain, w_blocks, tm):
    T, D = x2.shape
    nb, _, tn = w_blocks.shape
    W = nb * tn
    return pl.pallas_call(
        _inproj_kernel,
        grid=(T // tm, W // tn),
        in_specs=[
            pl.BlockSpec((tm, D), lambda i, j: (i, 0)),
            pl.BlockSpec((1, D), lambda i, j: (0, 0)),
            pl.BlockSpec((None, D, tn), lambda i, j: (j, 0, 0)),
        ],
        out_specs=pl.BlockSpec((tm, tn), lambda i, j: (i, j)),
        out_shape=jax.ShapeDtypeStruct((T, W), BF16),
        scratch_shapes=[pltpu.VMEM((tm, D), BF16)],
        compiler_params=_params(("parallel", "arbitrary")),
        name="in_proj",
    )(x2, gain, w_blocks)


def _attn_kernel(sinks_ref, q_ref, kc_ref, kp_ref, vc_ref, vp_ref,
                 cos_ref, sin_ref, cosp_ref, sinp_ref, o_ref):
    n = pl.program_id(1)
    C = CHUNK
    cos, sin = cos_ref[...], sin_ref[...]
    cosp, sinp = cosp_ref[...], sinp_ref[...]
    qi = lax.broadcasted_iota(jnp.int32, (C, 2 * C), 0)
    kj = lax.broadcasted_iota(jnp.int32, (C, 2 * C), 1)
    diff = qi + C - kj
    in_band = jnp.where(diff >= 0, jnp.where(diff < WINDOW, 1, 0), 0)
    has_prev = jnp.where(n > 0, 1, 0)
    allowed = in_band * jnp.where(kj >= C, 1, has_prev) > 0

    for hk in range(ATTN_KV_HEADS):
        ksl = slice(hk * HEAD_DIM, (hk + 1) * HEAD_DIM)
        kc = _rope(kc_ref[:, ksl].astype(F32), cos, sin)
        kp = _rope(kp_ref[:, ksl].astype(F32), cosp, sinp)
        kb = jnp.concatenate([kp, kc], axis=0).astype(BF16)
        vb = jnp.concatenate([vp_ref[:, ksl], vc_ref[:, ksl]], axis=0)
        for g in range(ATTN_GROUP):
            h = hk * ATTN_GROUP + g
            hsl = slice(h * HEAD_DIM, (h + 1) * HEAD_DIM)
            q = _rope(q_ref[:, hsl].astype(F32), cos, sin).astype(BF16)
            s = lax.dot_general(q, kb, (((1,), (1,)), ((), ())), preferred_element_type=F32)
            s = jnp.where(allowed, s * (HEAD_DIM ** -0.5), -1e30)
            sink = sinks_ref[h]
            m = jnp.maximum(jnp.max(s, axis=-1, keepdims=True), sink)
            p = jnp.exp(s - m)
            denom = jnp.sum(p, axis=-1, keepdims=True) + jnp.exp(sink - m)
            o = jnp.dot(p.astype(BF16), vb, preferred_element_type=F32)
            o_ref[:, hsl] = (o / denom).astype(o_ref.dtype)


def _attention(proj, sinks, cos_full, sin_signed, B, S):
    C = CHUNK
    N = S // C
    qa_blk = COL_QA // (ATTN_Q_HEADS * HEAD_DIM)
    ka_blk = COL_KA // (ATTN_KV_HEADS * HEAD_DIM)
    va_blk = COL_VA // (ATTN_KV_HEADS * HEAD_DIM)
    kvw = ATTN_KV_HEADS * HEAD_DIM
    cur = lambda b, n: b * N + n
    prev = lambda b, n: b * N + jnp.maximum(n - 1, 0)
    return pl.pallas_call(
        _attn_kernel,
        grid=(B, N),
        in_specs=[
            pl.BlockSpec(memory_space=pltpu.SMEM),
            pl.BlockSpec((C, ATTN_Q_HEADS * HEAD_DIM), lambda b, n: (cur(b, n), qa_blk)),
            pl.BlockSpec((C, kvw), lambda b, n: (cur(b, n), ka_blk)),
            pl.BlockSpec((C, kvw), lambda b, n: (prev(b, n), ka_blk)),
            pl.BlockSpec((C, kvw), lambda b, n: (cur(b, n), va_blk)),
            pl.BlockSpec((C, kvw), lambda b, n: (prev(b, n), va_blk)),
            pl.BlockSpec((C, HEAD_DIM), lambda b, n: (n, 0)),
            pl.BlockSpec((C, HEAD_DIM), lambda b, n: (n, 0)),
            pl.BlockSpec((C, HEAD_DIM), lambda b, n: (jnp.maximum(n - 1, 0), 0)),
            pl.BlockSpec((C, HEAD_DIM), lambda b, n: (jnp.maximum(n - 1, 0), 0)),
        ],
        out_specs=pl.BlockSpec((C, ATTN_Q_HEADS * HEAD_DIM), lambda b, n: (cur(b, n), 0)),
        out_shape=jax.ShapeDtypeStruct((B * S, ATTN_Q_HEADS * HEAD_DIM), BF16),
        compiler_params=_params(("parallel", "arbitrary")),
        name="swa_attention",
    )(sinks, proj, proj, proj, proj, proj, cos_full, sin_signed, cos_full, sin_signed)


def _ret_kernel(cdec_ref, q_ref, k_ref, v_ref, g_ref, cos_ref, sin_ref,
                dmask_ref, qdec_ref, kdec_ref, o_ref, state_ref):
    @pl.when(pl.program_id(1) == 0)
    def _():
        state_ref[...] = jnp.zeros_like(state_ref)

    cos, sin = cos_ref[...], sin_ref[...]
    for h in range(RET_HEADS):
        qsl = slice(h * RET_QK_DIM, (h + 1) * RET_QK_DIM)
        vsl = slice(h * RET_V_DIM, (h + 1) * RET_V_DIM)
        q = _rope(q_ref[:, qsl].astype(F32), cos, sin)
        k = _rope(k_ref[:, qsl].astype(F32), cos, sin) * (RET_QK_DIM ** -0.5)
        v = v_ref[:, vsl]
        inner = lax.dot_general(q.astype(BF16), k.astype(BF16), (((1,), (1,)), ((), ())),
                                preferred_element_type=F32) * dmask_ref[h]
        o = jnp.dot(inner.astype(BF16), v, preferred_element_type=F32)
        state = state_ref[h]
        o = o + jnp.dot((q * qdec_ref[h]).astype(BF16), state.astype(BF16),
                        preferred_element_type=F32)
        kd_t = (k * kdec_ref[h]).T.astype(BF16)
        state_ref[h] = state * cdec_ref[h] + jnp.dot(kd_t, v, preferred_element_type=F32)
        mu = jnp.mean(o, axis=-1, keepdims=True)
        oc = o - mu
        var = jnp.mean(oc * oc, axis=-1, keepdims=True)
        gate = g_ref[:, vsl].astype(F32)
        o_ref[:, vsl] = (oc * lax.rsqrt(var + NORM_EPS) * (gate * jax.nn.sigmoid(gate))).astype(o_ref.dtype)


def _retention(proj, cos_full, sin_signed, B, S):
    C = CHUNK
    N = S // C
    H = RET_HEADS
    log_gamma = jnp.log(1.0 - 2.0 ** (-5.0 - jnp.arange(H, dtype=F32)))
    pos = jnp.arange(C, dtype=F32)
    d = pos[:, None] - pos[None, :]
    dmask = jnp.where(d >= 0, jnp.exp(log_gamma[:, None, None] * jnp.maximum(d, 0.0)), 0.0)
    kdec = jnp.exp(log_gamma[:, None] * (C - 1.0 - pos)[None, :])
    qdec = jnp.exp(log_gamma[:, None] * (pos + 1.0)[None, :])
    cdec = jnp.exp(log_gamma * C)
    qdec_b = jnp.broadcast_to(qdec[:, :, None], (H, C, RET_QK_DIM))
    kdec_b = jnp.broadcast_to(kdec[:, :, None], (H, C, RET_QK_DIM))
    qkw, vw = H * RET_QK_DIM, H * RET_V_DIM
    row = lambda b, n: b * N + n
    full3 = lambda b, n: (0, 0, 0)
    return pl.pallas_call(
        _ret_kernel,
        grid=(B, N),
        in_specs=[
            pl.BlockSpec(memory_space=pltpu.SMEM),
            pl.BlockSpec((C, qkw), lambda b, n: (row(b, n), COL_QR // qkw)),
            pl.BlockSpec((C, qkw), lambda b, n: (row(b, n), COL_KR // qkw)),
            pl.BlockSpec((C, vw), lambda b, n: (row(b, n), COL_VR // vw)),
            pl.BlockSpec((C, vw), lambda b, n: (row(b, n), COL_GR // vw)),
            pl.BlockSpec((C, HEAD_DIM), lambda b, n: (n, 0)),
            pl.BlockSpec((C, HEAD_DIM), lambda b, n: (n, 0)),
            pl.BlockSpec((H, C, C), full3),
            pl.BlockSpec((H, C, RET_QK_DIM), full3),
            pl.BlockSpec((H, C, RET_QK_DIM), full3),
        ],
        out_specs=pl.BlockSpec((C, vw), lambda b, n: (row(b, n), 0)),
        out_shape=jax.ShapeDtypeStruct((B * S, vw), BF16),
        scratch_shapes=[pltpu.VMEM((H, RET_QK_DIM, RET_V_DIM), F32)],
        compiler_params=_params(("parallel", "arbitrary")),
        name="retention",
    )(cdec, proj, proj, proj, proj, cos_full, sin_signed, dmask, qdec_b, kdec_b)


def _merge_kernel(ya_ref, yr_ref, wa_ref, wr_ref, ga_ref, gr_ref, o_ref):
    pa = jnp.dot(ya_ref[...], wa_ref[...], preferred_element_type=F32)
    pr = jnp.dot(yr_ref[...], wr_ref[...], preferred_element_type=F32)
    ga = jax.nn.sigmoid(ga_ref[...].astype(F32))
    gr = jax.nn.sigmoid(gr_ref[...].astype(F32))
    o_ref[...] = (ga * pa + gr * pr).astype(o_ref.dtype)


def _merge(ya, yr, wa_bf, wr_bf, proj, tm, tn):
    T = ya.shape[0]
    D = wa_bf.shape[1]
    return pl.pallas_call(
        _merge_kernel,
        grid=(T // tm, D // tn),
        in_specs=[
            pl.BlockSpec((tm, ya.shape[1]), lambda i, j: (i, 0)),
            pl.BlockSpec((tm, yr.shape[1]), lambda i, j: (i, 0)),
            pl.BlockSpec((wa_bf.shape[0], tn), lambda i, j: (0, j)),
            pl.BlockSpec((wr_bf.shape[0], tn), lambda i, j: (0, j)),
            pl.BlockSpec((tm, tn), lambda i, j: (i, COL_GA // tn + j)),
            pl.BlockSpec((tm, tn), lambda i, j: (i, COL_GRT // tn + j)),
        ],
        out_specs=pl.BlockSpec((tm, tn), lambda i, j: (i, j)),
        out_shape=jax.ShapeDtypeStruct((T, D), BF16),
        compiler_params=_params(("parallel", "arbitrary")),
        name="merge_branches",
    )(ya, yr, wa_bf, wr_bf, proj, proj)


def _outproj_kernel(x_ref, m_ref, w_ref, g_ref, h_ref, hnt_ref):
    h = x_ref[...] + jnp.dot(m_ref[...], w_ref[...], preferred_element_type=F32)
    h_ref[...] = h
    ms = jnp.mean(h * h, axis=-1, keepdims=True)
    hn = h * lax.rsqrt(ms + NORM_EPS) * g_ref[...]
    hnt_ref[...] = hn.T.astype(hnt_ref.dtype)


def _outproj(x2, merged, w_bf, gain, tm):
    T, D = x2.shape
    return pl.pallas_call(
        _outproj_kernel,
        grid=(T // tm,),
        in_specs=[
            pl.BlockSpec((tm, D), lambda i: (i, 0)),
            pl.BlockSpec((tm, D), lambda i: (i, 0)),
            pl.BlockSpec((D, D), lambda i: (0, 0)),
            pl.BlockSpec((1, D), lambda i: (0, 0)),
        ],
        out_specs=[
            pl.BlockSpec((tm, D), lambda i: (i, 0)),
            pl.BlockSpec((D, tm), lambda i: (0, i)),
        ],
        out_shape=[
            jax.ShapeDtypeStruct((T, D), F32),
            jax.ShapeDtypeStruct((D, T), BF16),
        ],
        compiler_params=_params(("parallel",)),
        name="out_proj",
    )(x2, merged, w_bf, gain)


def _staircase():
    K = PEER_TOPK
    return [(k1, k2) for k1 in range(K) for k2 in range(K) if (k1 + 1) * (k2 + 1) <= K + 1]


def _batcher_pairs(n):
    pairs, p = [], 1
    while p < n:
        k = p
        while k >= 1:
            for j in range(k % p, n - k, 2 * k):
                for i in range(min(k, n - j - k)):
                    if (i + j) // (2 * p) == (i + j + k) // (2 * p):
                        pairs.append((i + j, i + j + k))
            k //= 2
        p *= 2
    return pairs


def _compare_exchange(v, i, j):
    v[i], v[j] = jnp.maximum(v[i], v[j]), jnp.minimum(v[i], v[j])


def _sort_desc_padded(vals):
    n = 1
    while n < len(vals):
        n *= 2
    v = list(vals) + [None] * (n - len(vals))
    for i, j in _batcher_pairs(n):
        if v[j] is None:
            continue
        if v[i] is None:
            v[i], v[j] = v[j], None
            continue
        _compare_exchange(v, i, j)
    return v


def _sorted_top16(s_ref, ls):
    H, NK, K = PEER_HEADS, PEER_N_KEYS, PEER_TOPK
    sort_net = _batcher_pairs(K)
    top = None
    for g in range(NK // K):
        v = [s_ref[(g * K + i) * H:(g * K + i + 1) * H, ls] for i in range(K)]
        for i, j in sort_net:
            _compare_exchange(v, i, j)
        if top is None:
            top = v
            continue
        top = [jnp.maximum(top[i], v[K - 1 - i]) for i in range(K)]
        k = K // 2
        while k >= 1:
            for i in range(K):
                if i & k == 0:
                    _compare_exchange(top, i, i + k)
            k //= 2
    return top


def _routing_kernel(hnt_ref, wq_ref, kbig_ref, a_ref, b_ref, th_ref, s1o, s2o, bscr):
    H, NK, K = PEER_HEADS, PEER_N_KEYS, PEER_TOPK
    tb = hnt_ref.shape[1]
    half = H * PEER_HALF_DIM
    qt = jnp.dot(wq_ref[...], hnt_ref[...], preferred_element_type=F32).astype(BF16)
    s1o[...] = jnp.dot(kbig_ref[0], qt[:half], preferred_element_type=F32)
    s2o[...] = jnp.dot(kbig_ref[1], qt[half:], preferred_element_type=F32)

    pairs = _staircase()
    for lt in range(tb // LANES):
        ls = slice(lt * LANES, (lt + 1) * LANES)
        t1v = _sorted_top16(s1o, ls)
        t2v = _sorted_top16(s2o, ls)
        cand = [t1v[k1] + t2v[k2] for (k1, k2) in pairs]
        cmax = cand[0]
        tops = _sort_desc_padded(cand)[:K + 1]
        zsum = jnp.zeros((H, LANES), F32)
        for r in range(K):
            zsum = zsum + jnp.exp(tops[r] - cmax)
        inv_z = 1.0 / zsum
        th_ref[:, ls] = jnp.exp(0.5 * (tops[K - 1] + tops[K]) - cmax) * inv_z
        m1, m2 = t1v[0], t2v[0]
        for n in range(NK):
            rows = slice(n * H, (n + 1) * H)
            s1, s2 = s1o[rows, ls], s2o[rows, ls]
            a_ref[rows, ls] = jnp.where(s1 >= t1v[K - 1], jnp.exp(s1 - m1) * inv_z, 0.0)
            bscr[lt, rows, :] = jnp.where(s2 >= t2v[K - 1], jnp.exp(s2 - m2), 0.0)
        for h in range(H):
            b_ref[h * NK:(h + 1) * NK, ls] = bscr[lt, pl.ds(h, NK, stride=H), :]


def _routing(hnt, wq_t, kbig, tb):
    D, T = hnt.shape
    R = PEER_HEADS * PEER_N_KEYS
    scr = lambda rows: pltpu.VMEM((rows, tb), F32)
    return pl.pallas_call(
        _routing_kernel,
        grid=(T // tb,),
        in_specs=[
            pl.BlockSpec((D, tb), lambda i: (0, i)),
            pl.BlockSpec(wq_t.shape, lambda i: (0, 0), pipeline_mode=pl.Buffered(1)),
            pl.BlockSpec(kbig.shape, lambda i: (0, 0, 0), pipeline_mode=pl.Buffered(1)),
        ],
        out_specs=[
            pl.BlockSpec((R, tb), lambda i: (0, i)),
            pl.BlockSpec((R, tb), lambda i: (0, i)),
            pl.BlockSpec((PEER_HEADS, tb), lambda i: (0, i)),
        ],
        out_shape=[
            jax.ShapeDtypeStruct((R, T), F32),
            jax.ShapeDtypeStruct((R, T), F32),
            jax.ShapeDtypeStruct((PEER_HEADS, T), F32),
        ],
        scratch_shapes=[scr(R), scr(R), pltpu.VMEM((tb // LANES, R, LANES), F32)],
        compiler_params=_params(("parallel",)),
        name="peer_routing",
    )(hnt, wq_t, kbig)


def _gelu(x):
    return 0.5 * x * (1.0 + lax.erf(x * (2.0 ** -0.5)))


def _experts_kernel(hnt_ref, down_ref, upt_ref, a_ref, b_ref, th_ref, yt_ref,
                    arep, threp, p_scr, at_scr, *, sub, piece, m_split):
    H, NK, SL = PEER_HEADS, PEER_N_KEYS, SUBLANES
    ec, tb = down_ref.shape[0], hnt_ref.shape[1]

    @pl.when(pl.program_id(1) == 0)
    def _():
        yt_ref[...] = jnp.zeros_like(yt_ref)
        for h in range(H):
            threp[h] = jnp.broadcast_to(th_ref[h:h + 1, :], (SL, tb))

    for r in range(ec // NK * H):
        arep[r] = jnp.broadcast_to(a_ref[r:r + 1, :], (SL, tb))

    n_sub = ec // sub
    n_piece = tb // piece

    def down_proj(n):
        cols = slice(n * piece, (n + 1) * piece)
        for s in range(n_sub):
            at_scr[n % 2, s * sub:(s + 1) * sub, :] = jnp.dot(
                down_ref[s * sub:(s + 1) * sub, :], hnt_ref[:, cols], preferred_element_type=F32)

    def up_proj(n):
        cols = slice(n * piece, (n + 1) * piece)
        d_rows = upt_ref.shape[0] // m_split
        for m in range(m_split):
            rows = slice(m * d_rows, (m + 1) * d_rows)
            yt_ref[rows, cols] += jnp.dot(upt_ref[rows, :], p_scr[:, cols], preferred_element_type=F32)

    def gate_build(n):
        for s in range(n_sub):
            for c in range(sub // NK):
                i1 = s * (sub // NK) + c
                for lt in range(piece // LANES):
                    ls = slice(n * piece + lt * LANES, n * piece + (lt + 1) * LANES)
                    lp = slice(lt * LANES, (lt + 1) * LANES)
                    for g2 in range(NK // (2 * SL)):
                        tiles = []
                        for g in (2 * g2, 2 * g2 + 1):
                            w = jnp.zeros((SL, LANES), F32)
                            for h in range(H):
                                prod = arep[i1 * H + h, :, ls] * b_ref[h * NK + g * SL:h * NK + (g + 1) * SL, ls]
                                w = w + jnp.where(prod >= threp[h, :, ls], prod, 0.0)
                            r0 = s * sub + c * NK + g * SL
                            tiles.append(w * _gelu(at_scr[n % 2, r0:r0 + SL, lp]))
                        r0 = s * sub + c * NK + 2 * g2 * SL
                        p_scr[r0:r0 + 2 * SL, ls] = jnp.concatenate(tiles, axis=0).astype(BF16)

    down_proj(0)
    for n in range(n_piece):
        if n + 1 < n_piece:
            down_proj(n + 1)
        if n >= 1:
            up_proj(n - 1)
        gate_build(n)
    up_proj(n_piece - 1)


def _experts(hnt, down_bf, up, a_t, b_t, th_t, tb, ec, sub=512, piece=512, m_split=4):
    D, T = hnt.shape
    NE = down_bf.shape[0]
    a_rows = ec // PEER_N_KEYS * PEER_HEADS
    upt_bf = up.reshape(NE // ec, ec, D).transpose(0, 2, 1).astype(BF16)
    return pl.pallas_call(
        functools.partial(_experts_kernel, sub=sub, piece=piece, m_split=m_split),
        grid=(T // tb, NE // ec),
        in_specs=[
            pl.BlockSpec((D, tb), lambda t, e: (0, t), pipeline_mode=pl.Buffered(1)),
            pl.BlockSpec((ec, D), lambda t, e: (e, 0)),
            pl.BlockSpec((None, D, ec), lambda t, e: (e, 0, 0)),
            pl.BlockSpec((a_rows, tb), lambda t, e: (e, t)),
            pl.BlockSpec((b_t.shape[0], tb), lambda t, e: (0, t), pipeline_mode=pl.Buffered(1)),
            pl.BlockSpec((PEER_HEADS, tb), lambda t, e: (0, t)),
        ],
        out_specs=pl.BlockSpec((D, tb), lambda t, e: (0, t)),
        out_shape=jax.ShapeDtypeStruct((D, T), F32),
        scratch_shapes=[pltpu.VMEM((a_rows, SUBLANES, tb), F32),
                        pltpu.VMEM((PEER_HEADS, SUBLANES, tb), F32),
                        pltpu.VMEM((ec, tb), BF16),
                        pltpu.VMEM((min(2, tb // piece), ec, piece), F32)],
        compiler_params=_params(("parallel", "arbitrary"), EXPERTS_VMEM_LIMIT),
        name="peer_experts",
    )(hnt, down_bf, upt_bf, a_t, b_t, th_t)


def _finish_kernel(h_ref, yt_ref, g_ref, o_ref):
    h = h_ref[...] + yt_ref[...].T
    ms = jnp.mean(h * h, axis=-1, keepdims=True)
    o_ref[...] = h * lax.rsqrt(ms + NORM_EPS) * g_ref[...]


def _finish(h, yt, gain, tm):
    T, D = h.shape
    return pl.pallas_call(
        _finish_kernel,
        grid=(T // tm,),
        in_specs=[
            pl.BlockSpec((tm, D), lambda i: (i, 0)),
            pl.BlockSpec((D, tm), lambda i: (0, i)),
            pl.BlockSpec((1, D), lambda i: (0, 0)),
        ],
        out_specs=pl.BlockSpec((tm, D), lambda i: (i, 0)),
        out_shape=jax.ShapeDtypeStruct((T, D), F32),
        compiler_params=_params(("parallel",)),
        name="finish",
    )(h, yt, gain)


def _w_in_block_table(tn):
    src_off = {"qa": 0, "kv": 1024, "qr": 1536, "kr": 2560, "vr": 3584, "gr": 5632, "ga": 7680, "grt": 9728}
    width = {"qa": 1024, "kv": 512, "qr": 1024, "kr": 1024, "vr": 2048, "gr": 2048, "ga": 2048, "grt": 2048}
    table = []
    for name in ("vr", "gr", "ga", "grt", "qa", "qr", "kr", "kv"):
        assert src_off[name] % tn == 0 and width[name] % tn == 0
        table += [src_off[name] // tn + i for i in range(width[name] // tn)]
    return table


def _pack_kernel(w_ref, o_ref):
    o_ref[...] = w_ref[...].astype(o_ref.dtype)


def _pack_w_in(w, tn):
    D, W = w.shape
    table = _w_in_block_table(tn)

    def src_block(j):
        blk = jnp.int32(0)
        for k, t in enumerate(table):
            blk = jnp.where(j == k, t, blk)
        return blk

    return pl.pallas_call(
        _pack_kernel,
        grid=(W // tn,),
        in_specs=[pl.BlockSpec((D, tn), lambda j: (0, src_block(j)))],
        out_specs=pl.BlockSpec((None, D, tn), lambda j: (j, 0, 0)),
        out_shape=jax.ShapeDtypeStruct((W // tn, D, tn), BF16),
        compiler_params=_params(("parallel",)),
        name="pack_w_in",
    )(w)


def _rope_tables(S):
    pos = jnp.arange(S, dtype=F32)
    inv_freq = 1.0 / (ROPE_THETA ** (jnp.arange(0, HEAD_DIM, 2, dtype=F32) / HEAD_DIM))
    ang = pos[:, None] * inv_freq[None, :]
    cos, sin = jnp.cos(ang), jnp.sin(ang)
    return jnp.concatenate([cos, cos], axis=1), jnp.concatenate([-sin, sin], axis=1)


def kernel(x, attn_norm, w_in, attn_sinks, w_attn_branch, w_ret_branch, w_out, ffn_norm,
           w_peer_query, peer_sub_keys, peer_expert_down, peer_expert_up, final_norm):
    B, S, D = x.shape
    T = B * S
    assert w_in.shape[0] == 1, "single-layer block"
    assert S % CHUNK == 0 and T % 512 == 0

    x2 = x.reshape(T, D)
    cos_full, sin_signed = _rope_tables(S)

    proj = _inproj(x2, attn_norm[0][None, :], _pack_w_in(w_in[0], tn=512), tm=1024)
    y_a = _attention(proj, attn_sinks[0], cos_full, sin_signed, B, S)
    y_r = _retention(proj, cos_full, sin_signed, B, S)
    merged = _merge(y_a, y_r, w_attn_branch[0].astype(BF16), w_ret_branch[0].astype(BF16),
                    proj, tm=1024, tn=512)
    h, hnt = _outproj(x2, merged, w_out[0].astype(BF16), ffn_norm[0][None, :], tm=256)

    H, NK, HD = PEER_HEADS, PEER_N_KEYS, PEER_HALF_DIM
    wq_t = w_peer_query[0].reshape(D, H, 2, HD).transpose(2, 1, 3, 0).reshape(2 * H * HD, D).astype(BF16)
    eye = jnp.eye(H, dtype=F32)
    kbig = jnp.einsum("phnd,hg->pnhgd", peer_sub_keys[0], eye).reshape(2, NK * H, H * HD).astype(BF16)
    a_t, b_t, th_t = _routing(hnt, wq_t, kbig, tb=512)

    yt = _experts(hnt, peer_expert_down[0].astype(BF16), peer_expert_up[0],
                  a_t, b_t, th_t, tb=1024, ec=1024)
    out = _finish(h, yt, final_norm[None, :], tm=256)
    return out.reshape(B, S, D)
```

```python
import functools

import jax
import jax.numpy as jnp
from jax import lax
from jax.experimental import pallas as pl
from jax.experimental.pallas import tpu as pltpu

F32 = jnp.float32
BF16 = jnp.bfloat16

HEAD_DIM = 128
ATTN_Q_HEADS = 8
ATTN_KV_HEADS = 2
ATTN_GROUP = ATTN_Q_HEADS // ATTN_KV_HEADS
WINDOW = 128
CHUNK = 128
ROPE_THETA = 10000.0
RET_HEADS = 8
RET_QK_DIM = 128
RET_V_DIM = 256
PEER_HEADS = 8
PEER_N_KEYS = 128
PEER_TOPK = 16
PEER_HALF_DIM = 128
NORM_EPS = 1e-6
NEG_INF = float("-inf")

LANES = 128
SUBLANES = 8
VMEM_LIMIT = 48 * 1024 * 1024
EXPERTS_VMEM_LIMIT = 58 * 1024 * 1024

SEG_W2048 = 2048
COL_VR, COL_GR, COL_GA, COL_GRT = 0, 2048, 4096, 6144
COL_QA, COL_QR, COL_KR = 8192, 9216, 10240
COL_KA, COL_VA = 11264, 11520
IN_WIDTH = 11776


def _params(sem, vmem_limit=VMEM_LIMIT):
    return pltpu.CompilerParams(dimension_semantics=sem, vmem_limit_bytes=vmem_limit)


def _rope(x, cos_full, sin_signed):
    return x * cos_full + pltpu.roll(x, HEAD_DIM // 2, axis=1) * sin_signed


def _inproj_kernel(x_ref, g_ref, w_ref, o_ref, xn_ref):
    @pl.when(pl.program_id(1) == 0)
    def _():
        x = x_ref[...]
        ms = jnp.mean(x * x, axis=-1, keepdims=True)
        xn_ref[...] = (x * lax.rsqrt(ms + NORM_EPS) * g_ref[...]).astype(BF16)

    o_ref[...] = jnp.dot(xn_ref[...], w_ref[...], preferred_element_type=F32).astype(o_ref.dtype)


def _inproj(x2, gain, w_blocks, tm):
    T, D = x2.shape
    nb, _, tn = w_blocks.shape
    W = nb * tn
    return pl.pallas_call(
        _inproj_kernel,
        grid=(T // tm, W // tn),
        in_specs=[
            pl.BlockSpec((tm, D), lambda i, j: (i, 0)),
            pl.BlockSpec((1, D), lambda i, j: (0, 0)),
            pl.BlockSpec((None, D, tn), lambda i, j: (j, 0, 0)),
        ],
        out_specs=pl.BlockSpec((tm, tn), lambda i, j: (i, j)),
        out_shape=jax.ShapeDtypeStruct((T, W), BF16),
        scratch_shapes=[pltpu.VMEM((tm, D), BF16)],
        compiler_params=_params(("parallel", "arbitrary")),
        name="in_proj",
    )(x2, gain, w_blocks)


def _attn_kernel(sinks_ref, q_ref, kc_ref, kp_ref, vc_ref, vp_ref,
                 cos_ref, sin_ref, cosp_ref, sinp_ref, o_ref):
    n = pl.program_id(1)
    C = CHUNK
    cos, sin = cos_ref[...], sin_ref[...]
    cosp, sinp = cosp_ref[...], sinp_ref[...]
    qi = lax.broadcasted_iota(jnp.int32, (C, 2 * C), 0)
    kj = lax.broadcasted_iota(jnp.int32, (C, 2 * C), 1)
    diff = qi + C - kj
    in_band = jnp.where(diff >= 0, jnp.where(diff < WINDOW, 1, 0), 0)
    has_prev = jnp.where(n > 0, 1, 0)
    allowed = in_band * jnp.where(kj >= C, 1, has_prev) > 0

    for hk in range(ATTN_KV_HEADS):
        ksl = slice(hk * HEAD_DIM, (hk + 1) * HEAD_DIM)
        kc = _rope(kc_ref[:, ksl].astype(F32), cos, sin)
        kp = _rope(kp_ref[:, ksl].astype(F32), cosp, sinp)
        kb = jnp.concatenate([kp, kc], axis=0).astype(BF16)
        vb = jnp.concatenate([vp_ref[:, ksl], vc_ref[:, ksl]], axis=0)
        for g in range(ATTN_GROUP):
            h = hk * ATTN_GROUP + g
            hsl = slice(h * HEAD_DIM, (h + 1) * HEAD_DIM)
            q = _rope(q_ref[:, hsl].astype(F32), cos, sin).astype(BF16)
            s = lax.dot_general(q, kb, (((1,), (1,)), ((), ())), preferred_element_type=F32)
            s = jnp.where(allowed, s * (HEAD_DIM ** -0.5), -1e30)
            sink = sinks_ref[h]
            m = jnp.maximum(jnp.max(s, axis=-1, keepdims=True), sink)
            p = jnp.exp(s - m)
            denom = jnp.sum(p, axis=-1, keepdims=True) + jnp.exp(sink - m)
            o = jnp.dot(p.astype(BF16), vb, preferred_element_type=F32)
            o_ref[:, hsl] = (o / denom).astype(o_ref.dtype)


def _attention(proj, sinks, cos_full, sin_signed, B, S):
    C = CHUNK
    N = S // C
    qa_blk = COL_QA // (ATTN_Q_HEADS * HEAD_DIM)
    ka_blk = COL_KA // (ATTN_KV_HEADS * HEAD_DIM)
    va_blk = COL_VA // (ATTN_KV_HEADS * HEAD_DIM)
    kvw = ATTN_KV_HEADS * HEAD_DIM
    cur = lambda b, n: b * N + n
    prev = lambda b, n: b * N + jnp.maximum(n - 1, 0)
    return pl.pallas_call(
        _attn_kernel,
        grid=(B, N),
        in_specs=[
            pl.BlockSpec(memory_space=pltpu.SMEM),
            pl.BlockSpec((C, ATTN_Q_HEADS * HEAD_DIM), lambda b, n: (cur(b, n), qa_blk)),
            pl.BlockSpec((C, kvw), lambda b, n: (cur(b, n), ka_blk)),
            pl.BlockSpec((C, kvw), lambda b, n: (prev(b, n), ka_blk)),
            pl.BlockSpec((C, kvw), lambda b, n: (cur(b, n), va_blk)),
            pl.BlockSpec((C, kvw), lambda b, n: (prev(b, n), va_blk)),
            pl.BlockSpec((C, HEAD_DIM), lambda b, n: (n, 0)),
            pl.BlockSpec((C, HEAD_DIM), lambda b, n: (n, 0)),
            pl.BlockSpec((C, HEAD_DIM), lambda b, n: (jnp.maximum(n - 1, 0), 0)),
            pl.BlockSpec((C, HEAD_DIM), lambda b, n: (jnp.maximum(n - 1, 0), 0)),
        ],
        out_specs=pl.BlockSpec((C, ATTN_Q_HEADS * HEAD_DIM), lambda b, n: (cur(b, n), 0)),
        out_shape=jax.ShapeDtypeStruct((B * S, ATTN_Q_HEADS * HEAD_DIM), BF16),
        compiler_params=_params(("parallel", "arbitrary")),
        name="swa_attention",
    )(sinks, proj, proj, proj, proj, proj, cos_full, sin_signed, cos_full, sin_signed)


def _ret_kernel(cdec_ref, q_ref, k_ref, v_ref, g_ref, cos_ref, sin_ref,
                dmask_ref, qdec_ref, kdec_ref, o_ref, state_ref):
    @pl.when(pl.program_id(1) == 0)
    def _():
        state_ref[...] = jnp.zeros_like(state_ref)

    cos, sin = cos_ref[...], sin_ref[...]
    for h in range(RET_HEADS):
        qsl = slice(h * RET_QK_DIM, (h + 1) * RET_QK_DIM)
        vsl = slice(h * RET_V_DIM, (h + 1) * RET_V_DIM)
        q = _rope(q_ref[:, qsl].astype(F32), cos, sin)
        k = _rope(k_ref[:, qsl].astype(F32), cos, sin) * (RET_QK_DIM ** -0.5)
        v = v_ref[:, vsl]
        inner = lax.dot_general(q.astype(BF16), k.astype(BF16), (((1,), (1,)), ((), ())),
                                preferred_element_type=F32) * dmask_ref[h]
        o = jnp.dot(inner.astype(BF16), v, preferred_element_type=F32)
        state = state_ref[h]
        o = o + jnp.dot((q * qdec_ref[h]).astype(BF16), state.astype(BF16),
                        preferred_element_type=F32)
        kd_t = (k * kdec_ref[h]).T.astype(BF16)
        state_ref[h] = state * cdec_ref[h] + jnp.dot(kd_t, v, preferred_element_type=F32)
        mu = jnp.mean(o, axis=-1, keepdims=True)
        oc = o - mu
        var = jnp.mean(oc * oc, axis=-1, keepdims=True)
        gate = g_ref[:, vsl].astype(F32)
        o_ref[:, vsl] = (oc * lax.rsqrt(var + NORM_EPS) * (gate * jax.nn.sigmoid(gate))).astype(o_ref.dtype)


def _retention(proj, cos_full, sin_signed, B, S):
    C = CHUNK
    N = S // C
    H = RET_HEADS
    log_gamma = jnp.log(1.0 - 2.0 ** (-5.0 - jnp.arange(H, dtype=F32)))
    pos = jnp.arange(C, dtype=F32)
    d = pos[:, None] - pos[None, :]
    dmask = jnp.where(d >= 0, jnp.exp(log_gamma[:, None, None] * jnp.maximum(d, 0.0)), 0.0)
    kdec = jnp.exp(log_gamma[:, None] * (C - 1.0 - pos)[None, :])
    qdec = jnp.exp(log_gamma[:, None] * (pos + 1.0)[None, :])
    cdec = jnp.exp(log_gamma * C)
    qdec_b = jnp.broadcast_to(qdec[:, :, None], (H, C, RET_QK_DIM))
    kdec_b = jnp.broadcast_to(kdec[:, :, None], (H, C, RET_QK_DIM))
    qkw, vw = H * RET_QK_DIM, H * RET_V_DIM
    row = lambda b, n: b * N + n
    full3 = lambda b, n: (0, 0, 0)
    return pl.pallas_call(
        _ret_kernel,
        grid=(B, N),
        in_specs=[
            pl.BlockSpec(memory_space=pltpu.SMEM),
            pl.BlockSpec((C, qkw), lambda b, n: (row(b, n), COL_QR // qkw)),
            pl.BlockSpec((C, qkw), lambda b, n: (row(b, n), COL_KR // qkw)),
            pl.BlockSpec((C, vw), lambda b, n: (row(b, n), COL_VR // vw)),
            pl.BlockSpec((C, vw), lambda b, n: (row(b, n), COL_GR // vw)),
            pl.BlockSpec((C, HEAD_DIM), lambda b, n: (n, 0)),
            pl.BlockSpec((C, HEAD_DIM), lambda b, n: (n, 0)),
            pl.BlockSpec((H, C, C), full3),
            pl.BlockSpec((H, C, RET_QK_DIM), full3),
            pl.BlockSpec((H, C, RET_QK_DIM), full3),
        ],
        out_specs=pl.BlockSpec((C, vw), lambda b, n: (row(b, n), 0)),
        out_shape=jax.ShapeDtypeStruct((B * S, vw), BF16),
        scratch_shapes=[pltpu.VMEM((H, RET_QK_DIM, RET_V_DIM), F32)],
        compiler_params=_params(("parallel", "arbitrary")),
        name="retention",
    )(cdec, proj, proj, proj, proj, cos_full, sin_signed, dmask, qdec_b, kdec_b)


def _merge_kernel(ya_ref, yr_ref, wa_ref, wr_ref, ga_ref, gr_ref, o_ref):
    pa = jnp.dot(ya_ref[...], wa_ref[...], preferred_element_type=F32)
    pr = jnp.dot(yr_ref[...], wr_ref[...], preferred_element_type=F32)
    ga = jax.nn.sigmoid(ga_ref[...].astype(F32))
    gr = jax.nn.sigmoid(gr_ref[...].astype(F32))
    o_ref[...] = (ga * pa + gr * pr).astype(o_ref.dtype)


def _merge(ya, yr, wa_bf, wr_bf, proj, tm, tn):
    T = ya.shape[0]
    D = wa_bf.shape[1]
    return pl.pallas_call(
        _merge_kernel,
        grid=(T // tm, D // tn),
        in_specs=[
            pl.BlockSpec((tm, ya.shape[1]), lambda i, j: (i, 0)),
            pl.BlockSpec((tm, yr.shape[1]), lambda i, j: (i, 0)),
            pl.BlockSpec((wa_bf.shape[0], tn), lambda i, j: (0, j)),
            pl.BlockSpec((wr_bf.shape[0], tn), lambda i, j: (0, j)),
            pl.BlockSpec((tm, tn), lambda i, j: (i, COL_GA // tn + j)),
            pl.BlockSpec((tm, tn), lambda i, j: (i, COL_GRT // tn + j)),
        ],
        out_specs=pl.BlockSpec((tm, tn), lambda i, j: (i, j)),
        out_shape=jax.ShapeDtypeStruct((T, D), BF16),
        compiler_params=_params(("parallel", "arbitrary")),
        name="merge_branches",
    )(ya, yr, wa_bf, wr_bf, proj, proj)


def _outproj_kernel(x_ref, m_ref, w_ref, g_ref, h_ref, hnt_ref):
    h = x_ref[...] + jnp.dot(m_ref[...], w_ref[...], preferred_element_type=F32)
    h_ref[...] = h
    ms = jnp.mean(h * h, axis=-1, keepdims=True)
    hn = h * lax.rsqrt(ms + NORM_EPS) * g_ref[...]
    hnt_ref[...] = hn.T.astype(hnt_ref.dtype)


def _outproj(x2, merged, w_bf, gain, tm):
    T, D = x2.shape
    return pl.pallas_call(
        _outproj_kernel,
        grid=(T // tm,),
        in_specs=[
            pl.BlockSpec((tm, D), lambda i: (i, 0)),
            pl.BlockSpec((tm, D), lambda i: (i, 0)),
            pl.BlockSpec((D, D), lambda i: (0, 0)),
            pl.BlockSpec((1, D), lambda i: (0, 0)),
        ],
        out_specs=[
            pl.BlockSpec((tm, D), lambda i: (i, 0)),
            pl.BlockSpec((D, tm), lambda i: (0, i)),
        ],
        out_shape=[
            jax.ShapeDtypeStruct((T, D), F32),
            jax.ShapeDtypeStruct((D, T), BF16),
        ],
        compiler_params=_params(("parallel",)),
        name="out_proj",
    )(x2, merged, w_bf, gain)


def _staircase():
    K = PEER_TOPK
    return [(k1, k2) for k1 in range(K) for k2 in range(K) if (k1 + 1) * (k2 + 1) <= K + 1]


def _batcher_pairs(n):
    pairs, p = [], 1
    while p < n:
        k = p
        while k >= 1:
            for j in range(k % p, n - k, 2 * k):
                for i in range(min(k, n - j - k)):
                    if (i + j) // (2 * p) == (i + j + k) // (2 * p):
                        pairs.append((i + j, i + j + k))
            k //= 2
        p *= 2
    return pairs


def _compare_exchange(v, i, j):
    v[i], v[j] = jnp.maximum(v[i], v[j]), jnp.minimum(v[i], v[j])


def _sort_desc_padded(vals):
    n = 1
    while n < len(vals):
        n *= 2
    v = list(vals) + [None] * (n - len(vals))
    for i, j in _batcher_pairs(n):
        if v[j] is None:
            continue
        if v[i] is None:
            v[i], v[j] = v[j], None
            continue
        _compare_exchange(v, i, j)
    return v


def _sorted_top16(s_ref, ls):
    H, NK, K = PEER_HEADS, PEER_N_KEYS, PEER_TOPK
    sort_net = _batcher_pairs(K)
    top = None
    for g in range(NK // K):
        v = [s_ref[(g * K + i) * H:(g * K + i + 1) * H, ls] for i in range(K)]
        for i, j in sort_net:
            _compare_exchange(v, i, j)
        if top is None:
            top = v
            continue
        top = [jnp.maximum(top[i], v[K - 1 - i]) for i in range(K)]
        k = K // 2
        while k >= 1:
            for i in range(K):
                if i & k == 0:
                    _compare_exchange(top, i, i + k)
            k //= 2
    return top


def _routing_kernel(hnt_ref, wq_ref, kbig_ref, a_ref, b_ref, th_ref, s1o, s2o, bscr):
    H, NK, K = PEER_HEADS, PEER_N_KEYS, PEER_TOPK
    tb = hnt_ref.shape[1]
    half = H * PEER_HALF_DIM
    qt = jnp.dot(wq_ref[...], hnt_ref[...], preferred_element_type=F32).astype(BF16)
    s1o[...] = jnp.dot(kbig_ref[0], qt[:half], preferred_element_type=F32)
    s2o[...] = jnp.dot(kbig_ref[1], qt[half:], preferred_element_type=F32)

    pairs = _staircase()
    for lt in range(tb // LANES):
        ls = slice(lt * LANES, (lt + 1) * LANES)
        t1v = _sorted_top16(s1o, ls)
        t2v = _sorted_top16(s2o, ls)
        cand = [t1v[k1] + t2v[k2] for (k1, k2) in pairs]
        cmax = cand[0]
        tops = _sort_desc_padded(cand)[:K + 1]
        zsum = jnp.zeros((H, LANES), F32)
        for r in range(K):
            zsum = zsum + jnp.exp(tops[r] - cmax)
        inv_z = 1.0 / zsum
        th_ref[:, ls] = jnp.exp(0.5 * (tops[K - 1] + tops[K]) - cmax) * inv_z
        m1, m2 = t1v[0], t2v[0]
        for n in range(NK):
            rows = slice(n * H, (n + 1) * H)
            s1, s2 = s1o[rows, ls], s2o[rows, ls]
            a_ref[rows, ls] = jnp.where(s1 >= t1v[K - 1], jnp.exp(s1 - m1) * inv_z, 0.0)
            bscr[lt, rows, :] = jnp.where(s2 >= t2v[K - 1], jnp.exp(s2 - m2), 0.0)
        for h in range(H):
            b_ref[h * NK:(h + 1) * NK, ls] = bscr[lt, pl.ds(h, NK, stride=H), :]


def _routing(hnt, wq_t, kbig, tb):
    D, T = hnt.shape
    R = PEER_HEADS * PEER_N_KEYS
    scr = lambda rows: pltpu.VMEM((rows, tb), F32)
    return pl.pallas_call(
        _routing_kernel,
        grid=(T // tb,),
        in_specs=[
            pl.BlockSpec((D, tb), lambda i: (0, i)),
            pl.BlockSpec(wq_t.shape, lambda i: (0, 0), pipeline_mode=pl.Buffered(1)),
            pl.BlockSpec(kbig.shape, lambda i: (0, 0, 0), pipeline_mode=pl.Buffered(1)),
        ],
        out_specs=[
            pl.BlockSpec((R, tb), lambda i: (0, i)),
            pl.BlockSpec((R, tb), lambda i: (0, i)),
            pl.BlockSpec((PEER_HEADS, tb), lambda i: (0, i)),
        ],
        out_shape=[
            jax.ShapeDtypeStruct((R, T), F32),
            jax.ShapeDtypeStruct((R, T), F32),
            jax.ShapeDtypeStruct((PEER_HEADS, T), F32),
        ],
        scratch_shapes=[scr(R), scr(R), pltpu.VMEM((tb // LANES, R, LANES), F32)],
        compiler_params=_params(("parallel",)),
        name="peer_routing",
    )(hnt, wq_t, kbig)


def _gelu(x):
    return 0.5 * x * (1.0 + lax.erf(x * (2.0 ** -0.5)))


def _experts_kernel(hnt_ref, down_ref, upt_ref, a_ref, b_ref, th_ref, yt_ref,
                    arep, threp, p_scr, *, sub, halves):
    H, NK, SL = PEER_HEADS, PEER_N_KEYS, SUBLANES
    ec, tb = down_ref.shape[0], hnt_ref.shape[1]

    @pl.when(pl.program_id(1) == 0)
    def _():
        yt_ref[...] = jnp.zeros_like(yt_ref)
        for h in range(H):
            threp[h] = jnp.broadcast_to(th_ref[h:h + 1, :], (SL, tb))

    for r in range(ec // NK * H):
        arep[r] = jnp.broadcast_to(a_ref[r:r + 1, :], (SL, tb))

    hnt = hnt_ref[...]
    n_sub = ec // sub
    for s in range(n_sub):
        at = jnp.dot(down_ref[s * sub:(s + 1) * sub, :], hnt, preferred_element_type=F32)
        for c in range(sub // NK):
            i1 = s * (sub // NK) + c
            for lt in range(tb // LANES):
                ls = slice(lt * LANES, (lt + 1) * LANES)
                for g2 in range(NK // (2 * SL)):
                    tiles = []
                    for g in (2 * g2, 2 * g2 + 1):
                        w = jnp.zeros((SL, LANES), F32)
                        for h in range(H):
                            prod = arep[i1 * H + h, :, ls] * b_ref[h * NK + g * SL:h * NK + (g + 1) * SL, ls]
                            w = w + jnp.where(prod >= threp[h, :, ls], prod, 0.0)
                        r0 = c * NK + g * SL
                        tiles.append(w * _gelu(at[r0:r0 + SL, ls]))
                    r0 = s * sub + c * NK + 2 * g2 * SL
                    p_scr[r0:r0 + 2 * SL, ls] = jnp.concatenate(tiles, axis=0).astype(BF16)
        if (s + 1) % (n_sub // halves) == 0:
            k0, k1 = (s + 1 - n_sub // halves) * sub, (s + 1) * sub
            yt_ref[...] += jnp.dot(upt_ref[:, k0:k1], p_scr[k0:k1, :], preferred_element_type=F32)


def _experts(hnt, down_bf, up, a_t, b_t, th_t, tb, ec, sub=512, halves=1):
    D, T = hnt.shape
    NE = down_bf.shape[0]
    a_rows = ec // PEER_N_KEYS * PEER_HEADS
    upt_bf = up.reshape(NE // ec, ec, D).transpose(0, 2, 1).astype(BF16)
    return pl.pallas_call(
        functools.partial(_experts_kernel, sub=sub, halves=halves),
        grid=(T // tb, NE // ec),
        in_specs=[
            pl.BlockSpec((D, tb), lambda t, e: (0, t), pipeline_mode=pl.Buffered(1)),
            pl.BlockSpec((ec, D), lambda t, e: (e, 0)),
            pl.BlockSpec((None, D, ec), lambda t, e: (e, 0, 0)),
            pl.BlockSpec((a_rows, tb), lambda t, e: (e, t)),
            pl.BlockSpec((b_t.shape[0], tb), lambda t, e: (0, t), pipeline_mode=pl.Buffered(1)),
            pl.BlockSpec((PEER_HEADS, tb), lambda t, e: (0, t)),
        ],
        out_specs=pl.BlockSpec((D, tb), lambda t, e: (0, t)),
        out_shape=jax.ShapeDtypeStruct((D, T), F32),
        scratch_shapes=[pltpu.VMEM((a_rows, SUBLANES, tb), F32),
                        pltpu.VMEM((PEER_HEADS, SUBLANES, tb), F32),
                        pltpu.VMEM((ec, tb), BF16)],
        compiler_params=_params(("parallel", "arbitrary"), EXPERTS_VMEM_LIMIT),
        name="peer_experts",
    )(hnt, down_bf, upt_bf, a_t, b_t, th_t)


def _finish_kernel(h_ref, yt_ref, g_ref, o_ref):
    h = h_ref[...] + yt_ref[...].T
    ms = jnp.mean(h * h, axis=-1, keepdims=True)
    o_ref[...] = h * lax.rsqrt(ms + NORM_EPS) * g_ref[...]


def _finish(h, yt, gain, tm):
    T, D = h.shape
    return pl.pallas_call(
        _finish_kernel,
        grid=(T // tm,),
        in_specs=[
            pl.BlockSpec((tm, D), lambda i: (i, 0)),
            pl.BlockSpec((D, tm), lambda i: (0, i)),
            pl.BlockSpec((1, D), lambda i: (0, 0)),
        ],
        out_specs=pl.BlockSpec((tm, D), lambda i: (i, 0)),
        out_shape=jax.ShapeDtypeStruct((T, D), F32),
        compiler_params=_params(("parallel",)),
        name="finish",
    )(h, yt, gain)


def _w_in_block_table(tn):
    src_off = {"qa": 0, "kv": 1024, "qr": 1536, "kr": 2560, "vr": 3584, "gr": 5632, "ga": 7680, "grt": 9728}
    width = {"qa": 1024, "kv": 512, "qr": 1024, "kr": 1024, "vr": 2048, "gr": 2048, "ga": 2048, "grt": 2048}
    table = []
    for name in ("vr", "gr", "ga", "grt", "qa", "qr", "kr", "kv"):
        assert src_off[name] % tn == 0 and width[name] % tn == 0
        table += [src_off[name] // tn + i for i in range(width[name] // tn)]
    return table


def _pack_kernel(w_ref, o_ref):
    o_ref[...] = w_ref[...].astype(o_ref.dtype)


def _pack_w_in(w, tn):
    D, W = w.shape
    table = _w_in_block_table(tn)

    def src_block(j):
        blk = jnp.int32(0)
        for k, t in enumerate(table):
            blk = jnp.where(j == k, t, blk)
        return blk

    return pl.pallas_call(
        _pack_kernel,
        grid=(W // tn,),
        in_specs=[pl.BlockSpec((D, tn), lambda j: (0, src_block(j)))],
        out_specs=pl.BlockSpec((None, D, tn), lambda j: (j, 0, 0)),
        out_shape=jax.ShapeDtypeStruct((W // tn, D, tn), BF16),
        compiler_params=_params(("parallel",)),
        name="pack_w_in",
    )(w)


def _rope_tables(S):
    pos = jnp.arange(S, dtype=F32)
    inv_freq = 1.0 / (ROPE_THETA ** (jnp.arange(0, HEAD_DIM, 2, dtype=F32) / HEAD_DIM))
    ang = pos[:, None] * inv_freq[None, :]
    cos, sin = jnp.cos(ang), jnp.sin(ang)
    return jnp.concatenate([cos, cos], axis=1), jnp.concatenate([-sin, sin], axis=1)


def kernel(x, attn_norm, w_in, attn_sinks, w_attn_branch, w_ret_branch, w_out, ffn_norm,
           w_peer_query, peer_sub_keys, peer_expert_down, peer_expert_up, final_norm):
    B, S, D = x.shape
    T = B * S
    assert w_in.shape[0] == 1, "single-layer block"
    assert S % CHUNK == 0 and T % 512 == 0

    x2 = x.reshape(T, D)
    cos_full, sin_signed = _rope_tables(S)

    proj = _inproj(x2, attn_norm[0][None, :], _pack_w_in(w_in[0], tn=512), tm=1024)
    y_a = _attention(proj, attn_sinks[0], cos_full, sin_signed, B, S)
    y_r = _retention(proj, cos_full, sin_signed, B, S)
    merged = _merge(y_a, y_r, w_attn_branch[0].astype(BF16), w_ret_branch[0].astype(BF16),
                    proj, tm=1024, tn=512)
    h, hnt = _outproj(x2, merged, w_out[0].astype(BF16), ffn_norm[0][None, :], tm=256)

    H, NK, HD = PEER_HEADS, PEER_N_KEYS, PEER_HALF_DIM
    wq_t = w_peer_query[0].reshape(D, H, 2, HD).transpose(2, 1, 3, 0).reshape(2 * H * HD, D).astype(BF16)
    eye = jnp.eye(H, dtype=F32)
    kbig = jnp.einsum("phnd,hg->pnhgd", peer_sub_keys[0], eye).reshape(2, NK * H, H * HD).astype(BF16)
    a_t, b_t, th_t = _routing(hnt, wq_t, kbig, tb=512)

    yt = _experts(hnt, peer_expert_down[0].astype(BF16), peer_expert_up[0],
                  a_t, b_t, th_t, tb=1024, ec=1024)
    out = _finish(h, yt, final_norm[None, :], tm=256)
    return out.reshape(B, S, D)
```
